```python
import math
import jax, jax.numpy as jnp
from jax import lax
import numpy as np

D_MODEL = 4096
BATCH = 2
SEQ = 4096
DEPTH = 4

D_FF = 3 * D_MODEL // 4
ATTN_HEADS = 8
ATTN_HEAD_DIM = 128
ATTN_V_DIM = 2 * ATTN_HEAD_DIM
ATTN_Q_BLOCK = 128
ROPE_THETA = 10000.0
GLA_HEADS = 4
GLA_K_DIM = 128
GLA_V_DIM = 256
GLA_GATE_RANK = 16
GLA_GATE_NORMALIZER = 16.0
HGRN_HEADS = 8
HGRN_K_DIM = 128
HGRN_V_DIM = 128
CHUNK = 64
NORM_EPS = 1e-6
LB_FLOOR = 1e-20

ATTN_QK_WIDTH = ATTN_HEADS * 2 * ATTN_HEAD_DIM
ATTN_WIDTH = ATTN_HEADS * ATTN_V_DIM
GLA_K_WIDTH = GLA_HEADS * GLA_K_DIM
GLA_WIDTH = GLA_HEADS * GLA_V_DIM
HGRN_K_WIDTH = HGRN_HEADS * HGRN_K_DIM
HGRN_WIDTH = HGRN_HEADS * HGRN_V_DIM

PROJ_SIZES = (
    ATTN_QK_WIDTH, ATTN_QK_WIDTH, ATTN_WIDTH,
    GLA_K_WIDTH, GLA_K_WIDTH, GLA_WIDTH, GLA_WIDTH,
    GLA_GATE_RANK, GLA_GATE_RANK,
    HGRN_K_WIDTH, HGRN_K_WIDTH, HGRN_K_WIDTH, HGRN_WIDTH, HGRN_WIDTH,
    D_MODEL, D_MODEL, D_MODEL,
)
PROJ_SPLITS = tuple(int(s) for s in np.cumsum(PROJ_SIZES)[:-1])
PROJ_TOTAL = int(sum(PROJ_SIZES))

kernel_name = "hybrid_diffattn_gla_hgrn2_macaron_encoder"


def rms_norm(x, gain):
    xf = x.astype(jnp.float32)
    y = xf * lax.rsqrt(jnp.mean(xf * xf, axis=-1, keepdims=True) + NORM_EPS)
    return (y * gain.astype(jnp.float32)).astype(x.dtype)


def swiglu_ffn(x, w_in, w_out):
    gate, up = jnp.split(x @ w_in, 2, axis=-1)
    return (jax.nn.silu(gate) * up) @ w_out


def rotary(x, positions):
    dh = x.shape[-1]
    half = dh // 2
    inv_freq = ROPE_THETA ** (-jnp.arange(half, dtype=jnp.float32) / half)
    ang = positions.astype(jnp.float32)[:, None] * inv_freq[None, :]
    cos = jnp.cos(ang)[:, None, None, :]
    sin = jnp.sin(ang)[:, None, None, :]
    xf = x.astype(jnp.float32)
    x1, x2 = xf[..., :half], xf[..., half:]
    return jnp.concatenate([x1 * cos - x2 * sin, x2 * cos + x1 * sin], axis=-1).astype(x.dtype)


def diff_attention(q, k, v, lam):
    b, s, h, _, dh = q.shape
    nb = s // ATTN_Q_BLOCK
    qb = q.reshape(b, nb, ATTN_Q_BLOCK, h, 2, dh).transpose(1, 0, 2, 3, 4, 5)
    kf = k.astype(jnp.float32)
    vf = v.astype(jnp.float32)
    scale = dh ** -0.5

    def one_block(q_blk):
        scores = jnp.einsum('bqhmd,bkhmd->bhmqk', q_blk.astype(jnp.float32), kf) * scale
        p = jax.nn.softmax(scores, axis=-1)
        a = p[:, :, 0] - lam * p[:, :, 1]
        return jnp.einsum('bhqk,bkhe->bqhe', a, vf)

    o = lax.map(one_block, qb)
    return o.transpose(1, 0, 2, 3, 4).reshape(b, s, h, -1).astype(v.dtype)


def chunked_gated_scan(q, k, v, log_f):
    b, h, s, dk = q.shape
    dv = v.shape[-1]
    n = s // CHUNK

    def chunks(t):
        return jnp.moveaxis(t.astype(jnp.float32).reshape(b, h, n, CHUNK, t.shape[-1]), 2, 0)

    mask = jnp.tril(jnp.ones((CHUNK, CHUNK), dtype=bool))[:, :, None]

    def step(state, inp):
        qc, kc, vc, gc = inp
        cum = jnp.cumsum(gc, axis=-2)
        diff = jnp.where(mask, cum[..., :, None, :] - cum[..., None, :, :], 0.0)
        decay = jnp.where(mask, jnp.exp(diff), 0.0)
        scores = jnp.einsum('bhik,bhjk,bhijk->bhij', qc, kc, decay)
        o = (jnp.einsum('bhij,bhjv->bhiv', scores, vc)
             + jnp.einsum('bhik,bhkv->bhiv', qc * jnp.exp(cum), state))
        last = cum[..., -1, :]
        state = (state * jnp.exp(last)[..., None]
                 + jnp.einsum('bhjk,bhjv->bhkv', kc * jnp.exp(last[..., None, :] - cum), vc))
        return state, o

    state0 = jnp.zeros((b, h, dk, dv), jnp.float32)
    _, o = lax.scan(step, state0, (chunks(q), chunks(k), chunks(v), chunks(log_f)))
    return jnp.moveaxis(o, 0, 2).reshape(b, h, s, dv).astype(v.dtype)


def bidirectional_scan(q, k_fwd, k_bwd, v, g_fwd, g_bwd):
    flip = lambda t: jnp.flip(t, axis=2)
    fwd = chunked_gated_scan(q, k_fwd, v, g_fwd)
    bwd = flip(chunked_gated_scan(flip(q), flip(k_bwd), flip(v), flip(g_bwd)))
    return fwd + bwd


def to_heads(t, n_heads):
    b, s, _ = t.shape
    return t.reshape(b, s, n_heads, -1).transpose(0, 2, 1, 3)


def layer_lower_bounds(logits):
    p = jax.nn.softmax(logits.astype(jnp.float32), axis=0)
    return jnp.cumsum(p, axis=0) - p[0]


def hgrn_gate(z, lb):
    zf = z.astype(jnp.float32)
    lb = jnp.clip(lb, 0.0, 1.0 - 1e-6)
    log_f = jnp.logaddexp(jnp.log(jnp.maximum(lb, LB_FLOOR)),
                          jnp.log1p(-lb) + jax.nn.log_sigmoid(zf))
    key_in = (1.0 - lb) * jax.nn.sigmoid(-zf)
    return key_in, log_f


def gla_log_gate(lr, w2, bias):
    return jax.nn.log_sigmoid((lr @ w2 + bias).astype(jnp.float32)) / GLA_GATE_NORMALIZER


def setup_inputs(seed: int = 0) -> dict:
    key = jax.random.key(seed)
    ks = jax.random.split(key, 32)
    L, D, F = DEPTH, D_MODEL, D_FF

    def w(k, shape, fan_in):
        return jax.random.normal(k, shape, jnp.float32) * (fan_in ** -0.5)

    def gain(k, shape):
        return 1.0 + 0.02 * jax.random.normal(k, shape, jnp.float32)

    def small(k, shape, s):
        return s * jax.random.normal(k, shape, jnp.float32)

    return {
        "x": jax.random.normal(ks[0], (BATCH, SEQ, D), jnp.float32),
        "ffn1_norm": gain(ks[1], (L, D)),
        "ffn1_w_in": w(ks[2], (L, D, 2 * F), D),
        "ffn1_w_out": w(ks[3], (L, F, D), F),
        "mix_norm": gain(ks[4], (L, D)),
        "w_in": w(ks[5], (L, D, PROJ_TOTAL), D),
        "attn_q_norm": gain(ks[6], (L, ATTN_HEAD_DIM)),
        "attn_k_norm": gain(ks[7], (L, ATTN_HEAD_DIM)),
        "attn_lambda": small(ks[8], (L, 4, ATTN_HEAD_DIM), 0.1),
        "attn_sub_norm": gain(ks[9], (L, ATTN_V_DIM)),
        "gla_gate_w2_fwd": w(ks[10], (L, GLA_GATE_RANK, GLA_K_WIDTH), GLA_GATE_RANK),
        "gla_gate_b_fwd": small(ks[11], (L, GLA_K_WIDTH), 0.01),
        "gla_gate_w2_bwd": w(ks[12], (L, GLA_GATE_RANK, GLA_K_WIDTH), GLA_GATE_RANK),
        "gla_gate_b_bwd": small(ks[13], (L, GLA_K_WIDTH), 0.01),
        "gla_out_norm": gain(ks[14], (L, GLA_V_DIM)),
        "hgrn_lb_fwd": small(ks[15], (L, HGRN_K_WIDTH), 0.5),
        "hgrn_lb_bwd": small(ks[16], (L, HGRN_K_WIDTH), 0.5),
        "hgrn_out_norm": gain(ks[17], (L, HGRN_V_DIM)),
        "w_branch_attn": w(ks[18], (L, ATTN_WIDTH, D), ATTN_WIDTH),
        "w_branch_gla": w(ks[19], (L, GLA_WIDTH, D), GLA_WIDTH),
        "w_branch_hgrn": w(ks[20], (L, HGRN_WIDTH, D), HGRN_WIDTH),
        "w_out": w(ks[21], (L, D, D), D),
        "ffn2_norm": gain(ks[22], (L, D)),
        "ffn2_w_in": w(ks[23], (L, D, 2 * F), D),
        "ffn2_w_out": w(ks[24], (L, F, D), F),
    }


def reference(x, ffn1_norm, ffn1_w_in, ffn1_w_out, mix_norm, w_in,
              attn_q_norm, attn_k_norm, attn_lambda, attn_sub_norm,
              gla_gate_w2_fwd, gla_gate_b_fwd, gla_gate_w2_bwd, gla_gate_b_bwd, gla_out_norm,
              hgrn_lb_fwd, hgrn_lb_bwd, hgrn_out_norm,
              w_branch_attn, w_branch_gla, w_branch_hgrn, w_out,
              ffn2_norm, ffn2_w_in, ffn2_w_out):
    b, s, _ = x.shape
    positions = jnp.arange(s, dtype=jnp.int32)
    lb_fwd_all = layer_lower_bounds(hgrn_lb_fwd)
    lb_bwd_all = layer_lower_bounds(hgrn_lb_bwd)

    for l in range(DEPTH):
        x = x + 0.5 * swiglu_ffn(rms_norm(x, ffn1_norm[l]), ffn1_w_in[l], ffn1_w_out[l])

        h = rms_norm(x, mix_norm[l])
        (a_q, a_k, a_v, g_q, g_k, g_v, g_r, g_lr_f, g_lr_b,
         h_q, h_zf, h_zb, h_i, h_g, gate_a, gate_g, gate_h) = jnp.split(h @ w_in[l], PROJ_SPLITS, axis=-1)

        q = a_q.reshape(b, s, ATTN_HEADS, 2, ATTN_HEAD_DIM)
        k = a_k.reshape(b, s, ATTN_HEADS, 2, ATTN_HEAD_DIM)
        q = rotary(rms_norm(q, attn_q_norm[l]), positions)
        k = rotary(rms_norm(k, attn_k_norm[l]), positions)
        v = a_v.reshape(b, s, ATTN_HEADS, ATTN_V_DIM)
        lam_vec = attn_lambda[l].astype(jnp.float32)
        lam_init = 0.8 - 0.6 * math.exp(-0.3 * l)
        lam = (jnp.exp(jnp.sum(lam_vec[0] * lam_vec[1]))
               - jnp.exp(jnp.sum(lam_vec[2] * lam_vec[3])) + lam_init)
        o_a = rms_norm(diff_attention(q, k, v, lam), attn_sub_norm[l]) * (1.0 - lam_init)
        y_a = o_a.reshape(b, s, ATTN_WIDTH) @ w_branch_attn[l]

        gq = to_heads(g_q, GLA_HEADS) * (GLA_K_DIM ** -0.5)
        gk = to_heads(g_k, GLA_HEADS)
        gv = to_heads(g_v, GLA_HEADS)
        gf = to_heads(gla_log_gate(g_lr_f, gla_gate_w2_fwd[l], gla_gate_b_fwd[l]), GLA_HEADS)
        gb = to_heads(gla_log_gate(g_lr_b, gla_gate_w2_bwd[l], gla_gate_b_bwd[l]), GLA_HEADS)
        o_g = bidirectional_scan(gq, gk, gk, gv, gf, gb).transpose(0, 2, 1, 3)
        o_g = rms_norm(o_g, gla_out_norm[l]).reshape(b, s, GLA_WIDTH) * jax.nn.silu(g_r)
        y_g = o_g @ w_branch_gla[l]

        k_f, logf_f = hgrn_gate(h_zf, lb_fwd_all[l])
        k_b, logf_b = hgrn_gate(h_zb, lb_bwd_all[l])
        hq = to_heads(h_q, HGRN_HEADS) * (HGRN_K_DIM ** -0.5)
        hi = to_heads(h_i, HGRN_HEADS)
        o_h = bidirectional_scan(hq, to_heads(k_f, HGRN_HEADS), to_heads(k_b, HGRN_HEADS), hi,
                                 to_heads(logf_f, HGRN_HEADS), to_heads(logf_b, HGRN_HEADS))
        o_h = rms_norm(o_h.transpose(0, 2, 1, 3), hgrn_out_norm[l]).reshape(b, s, HGRN_WIDTH) * jax.nn.silu(h_g)
        y_h = o_h @ w_branch_hgrn[l]

        merged = (jax.nn.sigmoid(gate_a) * y_a + jax.nn.sigmoid(gate_g) * y_g
                  + jax.nn.sigmoid(gate_h) * y_h)
        x = x + merged @ w_out[l]

        x = x + 0.5 * swiglu_ffn(rms_norm(x, ffn2_norm[l]), ffn2_w_in[l], ffn2_w_out[l])
    return x
```

```python
import functools
import math

import numpy as np
import jax
import jax.numpy as jnp
from jax import lax
from jax.experimental import pallas as pl
from jax.experimental.pallas import tpu as pltpu

F32 = jnp.float32
BF16 = jnp.bfloat16

NORM_EPS = 1e-6
LB_FLOOR = 1e-20
ROPE_THETA = 10000.0
GLA_GATE_NORMALIZER = 16.0

ATTN_HEADS = 8
ATTN_HEAD_DIM = 128
ATTN_V_DIM = 256
GLA_HEADS = 4
GLA_K_DIM = 128
GLA_V_DIM = 256
GLA_GATE_RANK = 16
HGRN_HEADS = 8
HGRN_K_DIM = 128
HGRN_V_DIM = 128

LANES = 128
VMEM_LIMIT_BYTES = 56 * 2**20
SCAN_CHUNK = 128
SCAN_BLOCK = 512


def _params(*sem):
    return pltpu.CompilerParams(dimension_semantics=sem, vmem_limit_bytes=VMEM_LIMIT_BYTES)


def _dot(a, b):
    return jnp.dot(a, b, preferred_element_type=F32)


def _dot_nt(a, b):
    return lax.dot_general(a, b, (((1,), (1,)), ((), ())), preferred_element_type=F32)


def _dot_tn(a, b):
    return lax.dot_general(a, b, (((0,), (0,)), ((), ())), preferred_element_type=F32)


def _sigmoid(x):
    return 1.0 / (1.0 + jnp.exp(-x))


def _silu(x):
    return x * _sigmoid(x)


def _log_sigmoid(x):
    return jnp.minimum(x, 0.0) - jnp.log1p(jnp.exp(-jnp.abs(x)))


def _split3(x):
    hi = x.astype(BF16)
    r1 = x - hi.astype(F32)
    mid = r1.astype(BF16)
    lo = (r1 - mid.astype(F32)).astype(BF16)
    return hi, mid, lo


def _rmsnorm_kernel(x_ref, g_ref, o_ref):
    x = x_ref[...]
    ms = jnp.mean(x * x, axis=-1, keepdims=True)
    o_ref[...] = (x * lax.rsqrt(ms + NORM_EPS) * g_ref[...]).astype(o_ref.dtype)


def _rmsnorm(x, gain, tm=256):
    t, d = x.shape
    return pl.pallas_call(
        _rmsnorm_kernel,
        grid=(t // tm,),
        in_specs=[pl.BlockSpec((tm, d), lambda i: (i, 0)),
                  pl.BlockSpec((1, d), lambda i: (0, 0))],
        out_specs=pl.BlockSpec((tm, d), lambda i: (i, 0)),
        out_shape=jax.ShapeDtypeStruct((t, d), BF16),
        compiler_params=_params("parallel"),
        name="rmsnorm",
    )(x, gain.reshape(1, d))


def _mm_kernel(*refs, n_a, pairs, epilogue):
    n_w = len(pairs)
    a_refs = refs[:n_a]
    w_refs = refs[n_a:n_a + n_w]
    e_refs = refs[n_a + n_w:-1]
    o_ref = refs[-1]
    accs = [_dot(a_refs[pairs[k]][...], w_refs[k][...]) for k in range(n_w)]
    o_ref[...] = epilogue(accs, [e[...] for e in e_refs]).astype(o_ref.dtype)


def _matmul(name, a_list, w_list, pairs, epilogue, n_out, out_dtype, tm, tn, extras=()):
    t = a_list[0].shape[0]
    in_specs, args = [], []
    for a in a_list:
        in_specs.append(pl.BlockSpec((tm, a.shape[1]), lambda j, i: (i, 0)))
        args.append(a)
    for w, off in w_list:
        in_specs.append(pl.BlockSpec((w.shape[0], tn), functools.partial(lambda j, i, o: (0, j + o), o=off)))
        args.append(w)
    for e, spec in extras:
        in_specs.append(spec)
        args.append(e)
    return pl.pallas_call(
        functools.partial(_mm_kernel, n_a=len(a_list), pairs=tuple(pairs), epilogue=epilogue),
        grid=(n_out // tn, t // tm),
        in_specs=in_specs,
        out_specs=pl.BlockSpec((tm, tn), lambda j, i: (i, j)),
        out_shape=jax.ShapeDtypeStruct((t, n_out), out_dtype),
        compiler_params=_params("parallel", "arbitrary"),
        name=name,
    )(*args)


def _tile_spec(tm, tn, off=0):
    return pl.BlockSpec((tm, tn), functools.partial(lambda j, i, o: (i, j + o), o=off))


def _row_spec(tn):
    return pl.BlockSpec((1, tn), lambda j, i: (0, j))


def _ep_plain(accs, extras):
    return accs[0]


def _ep_swiglu(accs, extras):
    return _silu(accs[0]) * accs[1]


def _ep_residual_half(accs, extras):
    return extras[0] + 0.5 * accs[0]


def _ep_residual(accs, extras):
    return extras[0] + accs[0]


def _ep_gated_sum(accs, extras):
    return (_sigmoid(extras[0]) * accs[0] + _sigmoid(extras[1]) * accs[1]
            + _sigmoid(extras[2]) * accs[2])


def _ep_qk_norm_rotary(accs, extras):
    acc = accs[0]
    gain, cosf, sinf = extras
    outs = []
    for c in range(acc.shape[1] // LANES):
        blk = acc[:, c * LANES:(c + 1) * LANES]
        ms = jnp.mean(blk * blk, axis=-1, keepdims=True)
        y = blk * lax.rsqrt(ms + NORM_EPS) * gain[:, c * LANES:(c + 1) * LANES]
        outs.append(y * cosf + pltpu.roll(y, LANES // 2, 1) * sinf)
    return jnp.concatenate(outs, axis=-1)


def _ffn(x, xn, w_in, w_out, tm=512, tn=1024):
    f = w_out.shape[0]
    d = w_out.shape[1]
    tf = 512
    h = _matmul("ffn_in", [xn], [(w_in, 0), (w_in, f // tf)], (0, 0), _ep_swiglu,
                f, BF16, tm, tf)
    return _matmul("ffn_out", [h], [(w_out, 0)], (0,), _ep_residual_half,
                   d, F32, tm, tn, extras=[(x, _tile_spec(tm, tn))])


def _attn_kernel(q_ref, k_ref, v_ref, lam_ref, gain_ref, linit_ref, o_ref):
    dh = ATTN_HEAD_DIM
    q = q_ref[...]
    k = k_ref[...]
    v = v_ref[...].astype(BF16)
    lv = lam_ref[...]
    lam_init = linit_ref[...]
    s01 = jnp.sum(lv[0:1] * lv[1:2], axis=-1, keepdims=True)
    s23 = jnp.sum(lv[2:3] * lv[3:4], axis=-1, keepdims=True)
    lam = jnp.exp(s01) - jnp.exp(s23) + lam_init

    def softmax_parts(m):
        s = _dot_nt(q[:, m * dh:(m + 1) * dh], k[:, m * dh:(m + 1) * dh])
        p = jnp.exp(s - jnp.max(s, axis=-1, keepdims=True))
        return p, 1.0 / jnp.sum(p, axis=-1, keepdims=True)

    p1, r1 = softmax_parts(0)
    p2, r2 = softmax_parts(1)
    a = p1 * r1 - p2 * (lam * r2)
    o = _dot(a.astype(BF16), v)
    ms = jnp.mean(o * o, axis=-1, keepdims=True)
    o = o * lax.rsqrt(ms + NORM_EPS) * gain_ref[...] * (1.0 - lam_init)
    o_ref[...] = o.astype(o_ref.dtype)


def _attention(qk, p_rest, v_off, lam_vec, sub_gain, lam_init, batch, seq, tq=256):
    t = qk.shape[0]
    hd = 2 * ATTN_HEAD_DIM
    nq = seq // tq
    return pl.pallas_call(
        _attn_kernel,
        grid=(batch, ATTN_HEADS, nq),
        in_specs=[
            pl.BlockSpec((tq, hd), lambda b, h, i: (b * nq + i, h)),
            pl.BlockSpec((seq, hd), lambda b, h, i: (b, ATTN_HEADS + h)),
            pl.BlockSpec((seq, ATTN_V_DIM), lambda b, h, i: (b, v_off + h)),
            pl.BlockSpec((4, ATTN_HEAD_DIM), lambda b, h, i: (0, 0)),
            pl.BlockSpec((1, ATTN_V_DIM), lambda b, h, i: (0, 0)),
            pl.BlockSpec((1, 1), lambda b, h, i: (0, 0)),
        ],
        out_specs=pl.BlockSpec((tq, ATTN_V_DIM), lambda b, h, i: (b * nq + i, h)),
        out_shape=jax.ShapeDtypeStruct((t, ATTN_HEADS * ATTN_V_DIM), BF16),
        compiler_params=_params("parallel", "parallel", "arbitrary"),
        name="diff_attention",
    )(qk, qk, p_rest, lam_vec, sub_gain.reshape(1, -1), lam_init.reshape(1, 1))


def _scan_constants(c, rev):
    idx = np.arange(c)
    if rev:
        cum = idx[None, :] >= idx[:, None]
    else:
        cum = idx[None, :] <= idx[:, None]
    mats = [cum.astype(np.float32)]
    valids = [np.eye(c, dtype=np.float32)]
    s = c // 2
    while s >= 1:
        blk = idx // (2 * s)
        late = (idx // s) % 2
        mid = blk * 2 * s + s
        d = np.zeros((c, c), np.float32)
        for i in range(c):
            if not rev:
                if late[i]:
                    d[i, mid[i] + 1:i + 1] = 1.0
                else:
                    d[i, i + 1:mid[i] + 1] = 1.0
            else:
                if late[i]:
                    d[i, mid[i]:i] = 1.0
                else:
                    d[i, i:mid[i]] = 1.0
        same = blk[:, None] == blk[None, :]
        if not rev:
            valid = same & (late[:, None] == 1) & (late[None, :] == 0)
        else:
            valid = same & (late[:, None] == 0) & (late[None, :] == 1)
        mats.append(d)
        valids.append(valid.astype(np.float32))
        s //= 2
    return (jnp.asarray(np.concatenate(mats, 0), dtype=BF16),
            jnp.asarray(np.stack(valids), dtype=F32))


def _scan_chunk(q, k, v, g, st_ref, dall_ref, valid_ref, rev):
    c = q.shape[0]
    n_lvl = valid_ref.shape[0] - 1
    dall = dall_ref[...]
    gh, gm, gl = _split3(g)
    e = _dot(dall, gh) + _dot(dall, gm) + _dot(dall, gl)
    cum = e[0:c]
    total = cum[0:1] if rev else cum[c - 1:c]
    row = lax.broadcasted_iota(jnp.int32, q.shape, 0)
    qb = q.astype(BF16)
    kb = k.astype(BF16)
    a = valid_ref[0] * _dot_nt(qb, kb)
    for lvl in range(1, n_lvl + 1):
        shift = n_lvl - lvl
        late = ((row >> shift) & 1) == 1
        q_rows = jnp.logical_not(late) if rev else late
        x = (jnp.where(q_rows, q, k) * jnp.exp(e[lvl * c:(lvl + 1) * c])).astype(BF16)
        a = a + valid_ref[lvl] * _dot_nt(x, x)
    st = st_ref[...]
    qd = (q * jnp.exp(cum)).astype(BF16)
    kd = (k * jnp.exp(total - cum)).astype(BF16)
    vb = v.astype(BF16)
    o = _dot(a.astype(BF16), vb) + _dot_nt(qd, st.astype(BF16))
    st_ref[...] = st * jnp.exp(total) + _dot_tn(vb, kd)
    return o


def _chunk_order(n, rev):
    return range(n - 1, -1, -1) if rev else range(n)


def _gla_kernel(q_ref, k_ref, v_ref, lr_ref, w2_ref, b_ref, dall_ref, valid_ref, o_ref, st_ref,
                *, rev, chunk):
    @pl.when(pl.program_id(2) == 0)
    def _():
        st_ref[...] = jnp.zeros_like(st_ref)

    r = GLA_GATE_RANK
    lr = lr_ref[...]
    lr = lr[:, r:2 * r] if rev else lr[:, 0:r]
    w2 = w2_ref[...]
    lr_hi = lr.astype(BF16)
    lr_lo = (lr - lr_hi.astype(F32)).astype(BF16)
    w2_hi = w2.astype(BF16)
    w2_lo = (w2 - w2_hi.astype(F32)).astype(BF16)
    z = _dot(lr_hi, w2_hi) + _dot(lr_hi, w2_lo) + _dot(lr_lo, w2_hi) + b_ref[...]
    g_all = _log_sigmoid(z) * (1.0 / GLA_GATE_NORMALIZER)
    scale = GLA_K_DIM ** -0.5
    for ci in _chunk_order(q_ref.shape[0] // chunk, rev):
        sl = slice(ci * chunk, (ci + 1) * chunk)
        o_ref[sl, :] = _scan_chunk(q_ref[sl, :] * scale, k_ref[sl, :], v_ref[sl, :], g_all[sl, :],
                                   st_ref, dall_ref, valid_ref, rev)


def _hgrn_kernel(q_ref, z_ref, v_ref, lbp_ref, lmask_ref, dall_ref, valid_ref, o_ref, st_ref,
                 *, rev, chunk):
    @pl.when(pl.program_id(2) == 0)
    def _():
        st_ref[...] = jnp.zeros_like(st_ref)

    logits = lbp_ref[...]
    pe = jnp.exp(logits - jnp.max(logits, axis=0, keepdims=True))
    p = pe / jnp.sum(pe, axis=0, keepdims=True)
    lb = jnp.sum(p * lmask_ref[...], axis=0, keepdims=True) - p[0:1]
    lb = jnp.clip(lb, 0.0, 1.0 - 1e-6)
    log_lb = jnp.log(jnp.maximum(lb, LB_FLOOR))
    log_1m = jnp.log1p(-lb)
    scale = HGRN_K_DIM ** -0.5
    for ci in _chunk_order(q_ref.shape[0] // chunk, rev):
        sl = slice(ci * chunk, (ci + 1) * chunk)
        z = z_ref[sl, :]
        t2 = log_1m + _log_sigmoid(z)
        hi = jnp.maximum(log_lb, t2)
        g = hi + jnp.log1p(jnp.exp(-jnp.abs(log_lb - t2)))
        k = (1.0 - lb) * _sigmoid(-z)
        o_ref[sl, :] = _scan_chunk(q_ref[sl, :] * scale, k, v_ref[sl, :], g,
                                   st_ref, dall_ref, valid_ref, rev)


def _scan_index(nblk, rev, col):
    if rev:
        return lambda b, h, c: (b * nblk + (nblk - 1 - c), col(h))
    return lambda b, h, c: (b * nblk + c, col(h))


def _gla_scan(p_rest, offs, lr, w2, bias, batch, seq, rev):
    q_off, k_off, v_off = offs
    t = p_rest.shape[0]
    tb, c = SCAN_BLOCK, SCAN_CHUNK
    nblk = seq // tb
    dall, valid = _scan_constants(c, rev)
    dk, dv = GLA_K_DIM, GLA_V_DIM
    const2 = lambda b, h, c_: (0, 0)
    return pl.pallas_call(
        functools.partial(_gla_kernel, rev=rev, chunk=c),
        grid=(batch, GLA_HEADS, nblk),
        in_specs=[
            pl.BlockSpec((tb, dk), _scan_index(nblk, rev, lambda h: q_off + h)),
            pl.BlockSpec((tb, dk), _scan_index(nblk, rev, lambda h: k_off + h)),
            pl.BlockSpec((tb, dv), _scan_index(nblk, rev, lambda h: v_off + h)),
            pl.BlockSpec((tb, 2 * GLA_GATE_RANK), _scan_index(nblk, rev, lambda h: 0)),
            pl.BlockSpec((GLA_GATE_RANK, dk), lambda b, h, c_: (0, h)),
            pl.BlockSpec((1, dk), lambda b, h, c_: (0, h)),
            pl.BlockSpec(dall.shape, const2),
            pl.BlockSpec(valid.shape, lambda b, h, c_: (0, 0, 0)),
        ],
        out_specs=pl.BlockSpec((tb, dv), _scan_index(nblk, rev, lambda h: h)),
        out_shape=jax.ShapeDtypeStruct((t, GLA_HEADS * dv), F32),
        scratch_shapes=[pltpu.VMEM((dv, dk), F32)],
        compiler_params=_params("parallel", "parallel", "arbitrary"),
        name="gla_scan_bwd" if rev else "gla_scan_fwd",
    )(p_rest, p_rest, p_rest, lr, w2, bias.reshape(1, -1), dall, valid)


def _hgrn_scan(p_rest, offs, lb_param, lmask, batch, seq, rev):
    q_off, z_off, v_off = offs
    t = p_rest.shape[0]
    tb, c = SCAN_BLOCK, SCAN_CHUNK
    nblk = seq // tb
    dall, valid = _scan_constants(c, rev)
    dk, dv = HGRN_K_DIM, HGRN_V_DIM
    n_layers = lb_param.shape[0]
    return pl.pallas_call(
        functools.partial(_hgrn_kernel, rev=rev, chunk=c),
        grid=(batch, HGRN_HEADS, nblk),
        in_specs=[
            pl.BlockSpec((tb, dk), _scan_index(nblk, rev, lambda h: q_off + h)),
            pl.BlockSpec((tb, dk), _scan_index(nblk, rev, lambda h: z_off + h)),
            pl.BlockSpec((tb, dv), _scan_index(nblk, rev, lambda h: v_off + h)),
            pl.BlockSpec((n_layers, dk), lambda b, h, c_: (0, h)),
            pl.BlockSpec((n_layers, 1), lambda b, h, c_: (0, 0)),
            pl.BlockSpec(dall.shape, lambda b, h, c_: (0, 0)),
            pl.BlockSpec(valid.shape, lambda b, h, c_: (0, 0, 0)),
        ],
        out_specs=pl.BlockSpec((tb, dv), _scan_index(nblk, rev, lambda h: h)),
        out_shape=jax.ShapeDtypeStruct((t, HGRN_HEADS * dv), F32),
        scratch_shapes=[pltpu.VMEM((dv, dk), F32)],
        compiler_params=_params("parallel", "parallel", "arbitrary"),
        name="hgrn_scan_bwd" if rev else "hgrn_scan_fwd",
    )(p_rest, p_rest, p_rest, lb_param, lmask, dall, valid)


def _headnorm_gate_kernel(of_ref, ob_ref, r_ref, gain_ref, o_ref, *, hd):
    o = of_ref[...] + ob_ref[...]
    gain = gain_ref[...]
    outs = []
    for h in range(o.shape[1] // hd):
        blk = o[:, h * hd:(h + 1) * hd]
        ms = jnp.mean(blk * blk, axis=-1, keepdims=True)
        outs.append(blk * lax.rsqrt(ms + NORM_EPS) * gain)
    y = jnp.concatenate(outs, axis=-1) * _silu(r_ref[...])
    o_ref[...] = y.astype(o_ref.dtype)


def _headnorm_gate(o_f, o_b, p_rest, r_off, gain, hd, tm=512):
    t, w = o_f.shape
    return pl.pallas_call(
        functools.partial(_headnorm_gate_kernel, hd=hd),
        grid=(t // tm,),
        in_specs=[pl.BlockSpec((tm, w), lambda i: (i, 0)),
                  pl.BlockSpec((tm, w), lambda i: (i, 0)),
                  pl.BlockSpec((tm, w), lambda i: (i, r_off)),
                  pl.BlockSpec((1, hd), lambda i: (0, 0))],
        out_specs=pl.BlockSpec((tm, w), lambda i: (i, 0)),
        out_shape=jax.ShapeDtypeStruct((t, w), BF16),
        compiler_params=_params("parallel"),
        name="headnorm_gate",
    )(o_f, o_b, p_rest, gain.reshape(1, hd))


def _rotary_tables(seq):
    half = ATTN_HEAD_DIM // 2
    inv_freq = ROPE_THETA ** (-jnp.arange(half, dtype=F32) / half)
    ang = jnp.arange(seq, dtype=jnp.int32).astype(F32)[:, None] * inv_freq[None, :]
    cos, sin = jnp.cos(ang), jnp.sin(ang)
    return jnp.concatenate([cos, cos], axis=-1), jnp.concatenate([-sin, sin], axis=-1)


def kernel(x, ffn1_norm, ffn1_w_in, ffn1_w_out, mix_norm, w_in, attn_q_norm, attn_k_norm, attn_lambda, attn_sub_norm, gla_gate_w2_fwd, gla_gate_b_fwd, gla_gate_w2_bwd, gla_gate_b_bwd, gla_out_norm, hgrn_lb_fwd, hgrn_lb_bwd, hgrn_out_norm, w_branch_attn, w_branch_gla, w_branch_hgrn, w_out, ffn2_norm, ffn2_w_in, ffn2_w_out):
    batch, seq, d = x.shape
    t = batch * seq
    depth = w_in.shape[0]
    cosf, sinf = _rotary_tables(seq)

    qk_w = 2 * ATTN_HEADS * 2 * ATTN_HEAD_DIM
    av_w = ATTN_HEADS * ATTN_V_DIM
    gla_w = 2 * GLA_HEADS * GLA_K_DIM + 2 * GLA_HEADS * GLA_V_DIM
    lr_w = 2 * GLA_GATE_RANK
    c_qk, c_av, c_gla = 0, qk_w, qk_w + av_w
    c_lr = c_gla + gla_w
    c_h = c_lr + lr_w
    hg_w = 3 * HGRN_HEADS * HGRN_K_DIM + 2 * HGRN_HEADS * HGRN_V_DIM
    c_gate = c_h + hg_w
    r_av, r_gla, r_h, r_gate = 0, av_w, av_w + gla_w, av_w + gla_w + hg_w
    n_rest = r_gate + 3 * d

    lam_inits = jnp.asarray([0.8 - 0.6 * math.exp(-0.3 * l) for l in range(depth)], F32)
    lmasks = jnp.asarray(np.tril(np.ones((depth, depth), np.float32)))[:, :, None]

    tm, tn = 512, 1024
    qk_scale = ATTN_HEAD_DIM ** -0.5

    def layer(xc, p):
        (f1n, f1wi, f1wo, mn, wi, aqn, akn, alam, asn, w2f, bf, w2b, bb, gon,
         hon, wba, wbg, wbh, wo, f2n, f2wi, f2wo, lam_init, lmask) = p

        xc = _ffn(xc, _rmsnorm(xc, f1n), f1wi.astype(BF16), f1wo.astype(BF16))

        hn = _rmsnorm(xc, mn)
        w_qk = wi[:, c_qk:c_qk + qk_w].astype(BF16)
        w_rest = jnp.concatenate([wi[:, c_av:c_lr], wi[:, c_h:]], axis=1).astype(BF16)
        w_lr = wi[:, c_lr:c_lr + lr_w].astype(BF16)

        n_sub = qk_w // (2 * ATTN_HEAD_DIM)
        qk_gain = jnp.concatenate([jnp.tile(aqn * qk_scale, n_sub), jnp.tile(akn, n_sub)]).reshape(1, qk_w)
        nrow = seq // tm
        rot_spec = pl.BlockSpec((tm, LANES), lambda j, i: (i % nrow, 0))
        qk = _matmul("proj_qk", [hn], [(w_qk, 0)], (0,), _ep_qk_norm_rotary, qk_w, BF16, tm, tn,
                     extras=[(qk_gain, _row_spec(tn)), (cosf, rot_spec), (sinf, rot_spec)])
        p_rest = _matmul("proj_rest", [hn], [(w_rest, 0)], (0,), _ep_plain, n_rest, F32, tm, tn)
        p_lr = _matmul("proj_lr", [hn], [(w_lr, 0)], (0,), _ep_plain, lr_w, F32, tm, lr_w)

        o_a = _attention(qk, p_rest, r_av // ATTN_V_DIM, alam, asn, lam_init, batch, seq)

        gq, gk = r_gla // GLA_K_DIM, (r_gla + GLA_HEADS * GLA_K_DIM) // GLA_K_DIM
        gv = (r_gla + 2 * GLA_HEADS * GLA_K_DIM) // GLA_V_DIM
        gr = (r_gla + 2 * GLA_HEADS * GLA_K_DIM + GLA_HEADS * GLA_V_DIM) // (GLA_HEADS * GLA_V_DIM)
        og_f = _gla_scan(p_rest, (gq, gk, gv), p_lr, w2f, bf, batch, seq, False)
        og_b = _gla_scan(p_rest, (gq, gk, gv), p_lr, w2b, bb, batch, seq, True)
        u_g = _headnorm_gate(og_f, og_b, p_rest, gr, gon, GLA_V_DIM)

        hw = HGRN_HEADS * HGRN_K_DIM
        hq, hzf, hzb, hi = (r_h // HGRN_K_DIM, (r_h + hw) // HGRN_K_DIM, (r_h + 2 * hw) // HGRN_K_DIM,
                            (r_h + 3 * hw) // HGRN_V_DIM)
        hgc = (r_h + 4 * hw) // (HGRN_HEADS * HGRN_V_DIM)
        oh_f = _hgrn_scan(p_rest, (hq, hzf, hi), hgrn_lb_fwd, lmask, batch, seq, False)
        oh_b = _hgrn_scan(p_rest, (hq, hzb, hi), hgrn_lb_bwd, lmask, batch, seq, True)
        u_h = _headnorm_gate(oh_f, oh_b, p_rest, hgc, hon, HGRN_V_DIM)

        g0 = r_gate // tn
        merged = _matmul(
            "merge", [o_a, u_g, u_h],
            [(wba.astype(BF16), 0), (wbg.astype(BF16), 0), (wbh.astype(BF16), 0)], (0, 1, 2),
            _ep_gated_sum, d, BF16, tm, tn,
            extras=[(p_rest, _tile_spec(tm, tn, g0)), (p_rest, _tile_spec(tm, tn, g0 + d // tn)),
                    (p_rest, _tile_spec(tm, tn, g0 + 2 * d // tn))])
        xc = _matmul("out_proj", [merged], [(wo.astype(BF16), 0)], (0,), _ep_residual, d, F32, tm, tn,
                     extras=[(xc, _tile_spec(tm, tn))])

        xc = _ffn(xc, _rmsnorm(xc, f2n), f2wi.astype(BF16), f2wo.astype(BF16))
        return xc, None

    per_layer = (ffn1_norm, ffn1_w_in, ffn1_w_out, mix_norm, w_in, attn_q_norm, attn_k_norm,
                 attn_lambda, attn_sub_norm, gla_gate_w2_fwd, gla_gate_b_fwd, gla_gate_w2_bwd,
                 gla_gate_b_bwd, gla_out_norm, hgrn_out_norm, w_branch_attn, w_branch_gla,
                 w_branch_hgrn, w_out, ffn2_norm, ffn2_w_in, ffn2_w_out, lam_inits, lmasks)
    out, _ = lax.scan(layer, x.reshape(t, d), per_layer)
    return out.reshape(batch, seq, d)
```

```python
import functools
import math

import numpy as np
import jax
import jax.numpy as jnp
from jax import lax
from jax.experimental import pallas as pl
from jax.experimental.pallas import tpu as pltpu

F32 = jnp.float32
BF16 = jnp.bfloat16

NORM_EPS = 1e-6
LB_FLOOR = 1e-20
ROPE_THETA = 10000.0
GLA_GATE_NORMALIZER = 16.0

ATTN_HEADS = 8
ATTN_HEAD_DIM = 128
ATTN_V_DIM = 256
GLA_HEADS = 4
GLA_K_DIM = 128
GLA_V_DIM = 256
GLA_GATE_RANK = 16
HGRN_HEADS = 8
HGRN_K_DIM = 128
HGRN_V_DIM = 128

LANES = 128
VMEM_LIMIT_BYTES = 56 * 2**20
SCAN_CHUNK = 128
SCAN_BLOCK = 512
LOG2_E = math.log2(math.e)


def _params(*sem):
    return pltpu.CompilerParams(dimension_semantics=sem, vmem_limit_bytes=VMEM_LIMIT_BYTES)


def _dot(a, b):
    return jnp.dot(a, b, preferred_element_type=F32)


def _dot_nt(a, b):
    return lax.dot_general(a, b, (((1,), (1,)), ((), ())), preferred_element_type=F32)


def _dot_tn(a, b):
    return lax.dot_general(a, b, (((0,), (0,)), ((), ())), preferred_element_type=F32)


def _sigmoid(x):
    return 1.0 / (1.0 + jnp.exp(-x))


def _silu(x):
    return x * _sigmoid(x)


def _log_sigmoid(x):
    return jnp.minimum(x, 0.0) - jnp.log1p(jnp.exp(-jnp.abs(x)))


def _split3(x):
    hi = x.astype(BF16)
    r1 = x - hi.astype(F32)
    mid = r1.astype(BF16)
    lo = (r1 - mid.astype(F32)).astype(BF16)
    return hi, mid, lo


def _rmsnorm_kernel(x_ref, g_ref, o_ref):
    x = x_ref[...]
    ms = jnp.mean(x * x, axis=-1, keepdims=True)
    o_ref[...] = (x * lax.rsqrt(ms + NORM_EPS) * g_ref[...]).astype(o_ref.dtype)


def _rmsnorm(x, gain, tm=256):
    t, d = x.shape
    return pl.pallas_call(
        _rmsnorm_kernel,
        grid=(t // tm,),
        in_specs=[pl.BlockSpec((tm, d), lambda i: (i, 0)),
                  pl.BlockSpec((1, d), lambda i: (0, 0))],
        out_specs=pl.BlockSpec((tm, d), lambda i: (i, 0)),
        out_shape=jax.ShapeDtypeStruct((t, d), BF16),
        compiler_params=_params("parallel"),
        name="rmsnorm",
    )(x, gain.reshape(1, d))


def _mm_kernel(l_ref, *refs, n_a, pairs, needs_cast, n_extra, n_split, epilogue):
    n_w = len(pairs)
    a_refs = refs[:n_a]
    w_refs = refs[n_a:n_a + n_w]
    e_refs = refs[n_a + n_w:n_a + n_w + n_extra]
    o_ref = refs[n_a + n_w + n_extra]
    scratch = refs[n_a + n_w + n_extra + 1:]
    w_bf = []
    si = 0
    for k in range(n_w):
        if needs_cast[k]:
            sc = scratch[si]
            si += 1

            @pl.when(pl.program_id(1) == 0)
            def _(sc=sc, w_ref=w_refs[k]):
                sc[...] = w_ref[...].astype(BF16)

            w_bf.append(sc)
        else:
            w_bf.append(w_refs[k])
    tn = o_ref.shape[1]
    sub = tn // n_split
    for c in range(n_split):
        cols = slice(c * sub, (c + 1) * sub)
        accs = [_dot(a_refs[pairs[k]][...], w_bf[k][:, cols]) for k in range(n_w)]
        ex = [e[:, cols] if e.shape[1] == tn else e[...] for e in e_refs]
        o_ref[:, cols] = epilogue(accs, ex).astype(o_ref.dtype)


def _matmul(name, layer, a_list, w_list, pairs, epilogue, n_out, out_dtype, tm, tn, extras=(),
            n_split=1):
    t = a_list[0].shape[0]
    in_specs, args, needs_cast, scratch = [], [], [], []
    for a in a_list:
        in_specs.append(pl.BlockSpec((tm, a.shape[1]), lambda j, i, l: (i, 0)))
        args.append(a)
    for w, off in w_list:
        in_specs.append(pl.BlockSpec((None, w.shape[1], tn),
                                     functools.partial(lambda j, i, l, o: (l[0], 0, j + o), o=off)))
        if w.dtype != BF16:
            scratch.append(pltpu.VMEM((w.shape[1], tn), BF16))
        needs_cast.append(w.dtype != BF16)
        args.append(w)
    for e, spec in extras:
        in_specs.append(spec)
        args.append(e)
    return pl.pallas_call(
        functools.partial(_mm_kernel, n_a=len(a_list), pairs=tuple(pairs), needs_cast=tuple(needs_cast),
                          n_extra=len(extras), n_split=n_split, epilogue=epilogue),
        grid_spec=pltpu.PrefetchScalarGridSpec(
            num_scalar_prefetch=1,
            grid=(n_out // tn, t // tm),
            in_specs=in_specs,
            out_specs=pl.BlockSpec((tm, tn), lambda j, i, l: (i, j)),
            scratch_shapes=scratch),
        out_shape=jax.ShapeDtypeStruct((t, n_out), out_dtype),
        compiler_params=_params("parallel", "arbitrary"),
        name=name,
    )(layer, *args)


def _tile_spec(tm, tn, off=0):
    return pl.BlockSpec((tm, tn), functools.partial(lambda j, i, l, o: (i, j + o), o=off))


def _row_spec(tn):
    return pl.BlockSpec((1, tn), lambda j, i, l: (0, j))


def _ep_plain(accs, extras):
    return accs[0]


def _ep_swiglu(accs, extras):
    return _silu(accs[0]) * accs[1]


def _ep_residual_half(accs, extras):
    return extras[0] + 0.5 * accs[0]


def _ep_residual(accs, extras):
    return extras[0] + accs[0]


def _ep_gated_sum(accs, extras):
    return (_sigmoid(extras[0]) * accs[0] + _sigmoid(extras[1]) * accs[1]
            + _sigmoid(extras[2]) * accs[2])


def _ep_qk_norm_rotary(accs, extras):
    acc = accs[0]
    gain, cosf, sinf = extras
    outs = []
    for c in range(acc.shape[1] // LANES):
        blk = acc[:, c * LANES:(c + 1) * LANES]
        ms = jnp.mean(blk * blk, axis=-1, keepdims=True)
        y = blk * lax.rsqrt(ms + NORM_EPS) * gain[:, c * LANES:(c + 1) * LANES]
        outs.append(y * cosf + pltpu.roll(y, LANES // 2, 1) * sinf)
    return jnp.concatenate(outs, axis=-1)


def _ffn(layer, x, xn, w_in, w_out, tm=1024):
    f, d = w_out.shape[1], w_out.shape[2]
    tf, tn = 256, 512
    h = _matmul("ffn_in", layer, [xn], [(w_in, 0), (w_in, f // tf)], (0, 0), _ep_swiglu,
                f, BF16, tm, tf)
    return _matmul("ffn_out", layer, [h], [(w_out, 0)], (0,), _ep_residual_half,
                   d, F32, tm, tn, extras=[(x, _tile_spec(tm, tn))])


def _attn_kernel(q_ref, k_ref, v_ref, lam_ref, gain_ref, linit_ref, o_ref):
    dh = ATTN_HEAD_DIM
    v = v_ref[...]
    lv = lam_ref[...]
    lam_init = linit_ref[...]
    s01 = jnp.sum(lv[0:1] * lv[1:2], axis=-1, keepdims=True)
    s23 = jnp.sum(lv[2:3] * lv[3:4], axis=-1, keepdims=True)
    lam = jnp.exp(s01) - jnp.exp(s23) + lam_init

    def one_map(m):
        s = _dot_nt(q_ref[:, m * dh:(m + 1) * dh], k_ref[:, m * dh:(m + 1) * dh])
        p = jnp.exp2(s - jnp.max(s, axis=-1, keepdims=True))
        inv_l = 1.0 / jnp.sum(p, axis=-1, keepdims=True)
        return _dot(p.astype(BF16), v), inv_l

    o1, r1 = one_map(0)
    o2, r2 = one_map(1)
    o = o1 * r1 - o2 * (lam * r2)
    ms = jnp.mean(o * o, axis=-1, keepdims=True)
    o = o * lax.rsqrt(ms + NORM_EPS) * gain_ref[...] * (1.0 - lam_init)
    o_ref[...] = o.astype(o_ref.dtype)


def _attention(qk, v, lam_vec, sub_gain, lam_init, batch, seq, tq=512):
    t = qk.shape[0]
    hd = 2 * ATTN_HEAD_DIM
    nq = seq // tq
    return pl.pallas_call(
        _attn_kernel,
        grid=(batch, ATTN_HEADS, nq),
        in_specs=[
            pl.BlockSpec((tq, hd), lambda b, h, i: (b * nq + i, h)),
            pl.BlockSpec((seq, hd), lambda b, h, i: (b, ATTN_HEADS + h)),
            pl.BlockSpec((seq, ATTN_V_DIM), lambda b, h, i: (b, h)),
            pl.BlockSpec((4, ATTN_HEAD_DIM), lambda b, h, i: (0, 0)),
            pl.BlockSpec((1, ATTN_V_DIM), lambda b, h, i: (0, 0)),
            pl.BlockSpec((1, 1), lambda b, h, i: (0, 0)),
        ],
        out_specs=pl.BlockSpec((tq, ATTN_V_DIM), lambda b, h, i: (b * nq + i, h)),
        out_shape=jax.ShapeDtypeStruct((t, ATTN_HEADS * ATTN_V_DIM), BF16),
        compiler_params=_params("parallel", "parallel", "arbitrary"),
        name="diff_attention",
    )(qk, qk, v, lam_vec, sub_gain.reshape(1, -1), lam_init.reshape(1, 1))


def _scan_constants(c, rev):
    idx = np.arange(c)
    if rev:
        cum = idx[None, :] >= idx[:, None]
    else:
        cum = idx[None, :] <= idx[:, None]
    valids = [np.eye(c, dtype=np.float32)]
    s = c // 2
    while s >= 1:
        blk = idx // (2 * s)
        late = (idx // s) % 2
        same = blk[:, None] == blk[None, :]
        if not rev:
            valid = same & (late[:, None] == 1) & (late[None, :] == 0)
        else:
            valid = same & (late[:, None] == 0) & (late[None, :] == 1)
        valids.append(valid.astype(np.float32))
        s //= 2
    return (jnp.asarray(cum.astype(np.float32), dtype=BF16),
            jnp.asarray(np.stack(valids), dtype=F32))


def _level_masks(c, dk, n_lvl, rev):
    row = lax.broadcasted_iota(jnp.int32, (c, dk), 0)
    masks = []
    for lvl in range(1, n_lvl + 1):
        late = ((row >> (n_lvl - lvl)) & 1) == 1
        masks.append(jnp.logical_not(late) if rev else late)
    return masks


def _mid_row(cum, s):
    c, dk = cum.shape
    if s >= 8:
        parts = [jnp.broadcast_to(cum[b * 2 * s + s:b * 2 * s + s + 1, :], (2 * s, dk))
                 for b in range(c // (2 * s))]
        return parts[0] if len(parts) == 1 else jnp.concatenate(parts, axis=0)
    x3 = cum.reshape(c // 8, 8, dk)
    sub = lax.broadcasted_iota(jnp.int32, x3.shape, 1)

    def row(r):
        return jnp.broadcast_to(x3[:, r:r + 1, :], x3.shape)

    if s == 4:
        r3 = row(4)
    elif s == 2:
        r3 = jnp.where(sub < 4, row(2), row(6))
    else:
        r3 = jnp.where(sub < 2, row(1), jnp.where(sub < 4, row(3), jnp.where(sub < 6, row(5), row(7))))
    return r3.reshape(c, dk)


def _scan_chunk(q, k, v, g, st, cum_ref, valid_ref, masks, rev):
    c = q.shape[0]
    n_lvl = len(masks)
    tri = cum_ref[...]
    gh, gm, gl = _split3(g)
    cum = _dot(tri, gh) + _dot(tri, gm) + _dot(tri, gl)
    total = cum[0:1] if rev else cum[c - 1:c]
    a = valid_ref[0] * _dot_nt(q.astype(BF16), k.astype(BF16))
    for lvl in range(1, n_lvl + 1):
        e = -jnp.abs(cum - _mid_row(cum, c >> lvl))
        x = (jnp.where(masks[lvl - 1], q, k) * jnp.exp(e)).astype(BF16)
        a = a + valid_ref[lvl] * _dot_nt(x, x)
    qd = (q * jnp.exp(cum)).astype(BF16)
    kd = (k * jnp.exp(total - cum)).astype(BF16)
    vb = v.astype(BF16)
    o = _dot(a.astype(BF16), vb) + _dot_nt(qd, st.astype(BF16))
    return o, st * jnp.exp(total) + _dot_tn(vb, kd)


def _gla_gate(lr, w2, bias):
    lr_hi = lr.astype(BF16)
    lr_lo = (lr - lr_hi.astype(F32)).astype(BF16)
    w2_hi = w2.astype(BF16)
    w2_lo = (w2 - w2_hi.astype(F32)).astype(BF16)
    z = _dot(lr_hi, w2_hi) + _dot(lr_hi, w2_lo) + _dot(lr_lo, w2_hi) + bias
    return _log_sigmoid(z) * (1.0 / GLA_GATE_NORMALIZER)


def _gla_kernel(qf_ref, kf_ref, vf_ref, lrf_ref, qb_ref, kb_ref, vb_ref, lrb_ref,
                w2f_ref, bf_ref, w2b_ref, bb_ref, dallf_ref, validf_ref, dallb_ref, validb_ref,
                of_ref, ob_ref, stf_ref, stb_ref, *, chunk):
    @pl.when(pl.program_id(2) == 0)
    def _():
        stf_ref[...] = jnp.zeros_like(stf_ref)
        stb_ref[...] = jnp.zeros_like(stb_ref)

    r = GLA_GATE_RANK
    gf_all = _gla_gate(lrf_ref[...][:, 0:r], w2f_ref[...], bf_ref[...])
    gb_all = _gla_gate(lrb_ref[...][:, r:2 * r], w2b_ref[...], bb_ref[...])
    n_lvl = validf_ref.shape[0] - 1
    mf = _level_masks(chunk, GLA_K_DIM, n_lvl, False)
    mb = _level_masks(chunk, GLA_K_DIM, n_lvl, True)
    scale = GLA_K_DIM ** -0.5
    n = qf_ref.shape[0] // chunk
    stf, stb = stf_ref[...], stb_ref[...]
    for ci in range(n):
        sf = slice(ci * chunk, (ci + 1) * chunk)
        sb = slice((n - 1 - ci) * chunk, (n - ci) * chunk)
        o, stf = _scan_chunk(qf_ref[sf, :] * scale, kf_ref[sf, :], vf_ref[sf, :], gf_all[sf, :],
                             stf, dallf_ref, validf_ref, mf, False)
        of_ref[sf, :] = o
        o, stb = _scan_chunk(qb_ref[sb, :] * scale, kb_ref[sb, :], vb_ref[sb, :], gb_all[sb, :],
                             stb, dallb_ref, validb_ref, mb, True)
        ob_ref[sb, :] = o
    stf_ref[...] = stf
    stb_ref[...] = stb


def _hgrn_lower_bound(logits, lmask):
    pe = jnp.exp(logits - jnp.max(logits, axis=0, keepdims=True))
    p = pe / jnp.sum(pe, axis=0, keepdims=True)
    lb = jnp.sum(p * lmask, axis=0, keepdims=True) - p[0:1]
    lb = jnp.clip(lb, 0.0, 1.0 - 1e-6)
    return lb, jnp.log(jnp.maximum(lb, LB_FLOOR)), jnp.log1p(-lb)


def _hgrn_gate(z, lb, log_lb, log_1m):
    t2 = log_1m + _log_sigmoid(z)
    g = jnp.maximum(log_lb, t2) + jnp.log1p(jnp.exp(-jnp.abs(log_lb - t2)))
    return (1.0 - lb) * _sigmoid(-z), g


def _hgrn_kernel(qf_ref, zf_ref, vf_ref, qb_ref, zb_ref, vb_ref, lbf_ref, lbb_ref, lmask_ref,
                 dallf_ref, validf_ref, dallb_ref, validb_ref, of_ref, ob_ref, stf_ref, stb_ref,
                 *, chunk):
    @pl.when(pl.program_id(2) == 0)
    def _():
        stf_ref[...] = jnp.zeros_like(stf_ref)
        stb_ref[...] = jnp.zeros_like(stb_ref)

    lbf = _hgrn_lower_bound(lbf_ref[...], lmask_ref[...])
    lbb = _hgrn_lower_bound(lbb_ref[...], lmask_ref[...])
    n_lvl = validf_ref.shape[0] - 1
    mf = _level_masks(chunk, HGRN_K_DIM, n_lvl, False)
    mb = _level_masks(chunk, HGRN_K_DIM, n_lvl, True)
    scale = HGRN_K_DIM ** -0.5
    n = qf_ref.shape[0] // chunk
    stf, stb = stf_ref[...], stb_ref[...]
    for ci in range(n):
        sf = slice(ci * chunk, (ci + 1) * chunk)
        sb = slice((n - 1 - ci) * chunk, (n - ci) * chunk)
        k, g = _hgrn_gate(zf_ref[sf, :], *lbf)
        o, stf = _scan_chunk(qf_ref[sf, :] * scale, k, vf_ref[sf, :], g,
                             stf, dallf_ref, validf_ref, mf, False)
        of_ref[sf, :] = o
        k, g = _hgrn_gate(zb_ref[sb, :], *lbb)
        o, stb = _scan_chunk(qb_ref[sb, :] * scale, k, vb_ref[sb, :], g,
                             stb, dallb_ref, validb_ref, mb, True)
        ob_ref[sb, :] = o
    stf_ref[...] = stf
    stb_ref[...] = stb


def _fwd_idx(nblk, col):
    return lambda b, h, c: (b * nblk + c, col(h))


def _bwd_idx(nblk, col):
    return lambda b, h, c: (b * nblk + (nblk - 1 - c), col(h))


def _const_spec(arr):
    zeros = (0,) * arr.ndim
    return pl.BlockSpec(arr.shape, lambda b, h, c: zeros)


def _gla_scan(p_gla, lr, w2f, bf, w2b, bb, batch, seq):
    t = p_gla.shape[0]
    tb, c = SCAN_BLOCK, SCAN_CHUNK
    nblk = seq // tb
    dk, dv, nh = GLA_K_DIM, GLA_V_DIM, GLA_HEADS
    consts = _scan_constants(c, False) + _scan_constants(c, True)
    tok_specs = []
    for idx in (_fwd_idx, _bwd_idx):
        tok_specs += [
            pl.BlockSpec((tb, dk), idx(nblk, lambda h: h)),
            pl.BlockSpec((tb, dk), idx(nblk, lambda h: nh + h)),
            pl.BlockSpec((tb, dv), idx(nblk, lambda h: (2 * nh * dk) // dv + h)),
            pl.BlockSpec((tb, 2 * GLA_GATE_RANK), idx(nblk, lambda h: 0)),
        ]
    gate_specs = [pl.BlockSpec((GLA_GATE_RANK, dk), lambda b, h, c_: (0, h)),
                  pl.BlockSpec((1, dk), lambda b, h, c_: (0, h))] * 2
    out_sd = jax.ShapeDtypeStruct((t, nh * dv), F32)
    return pl.pallas_call(
        functools.partial(_gla_kernel, chunk=c),
        grid=(batch, nh, nblk),
        in_specs=tok_specs + gate_specs + [_const_spec(a) for a in consts],
        out_specs=[pl.BlockSpec((tb, dv), _fwd_idx(nblk, lambda h: h)),
                   pl.BlockSpec((tb, dv), _bwd_idx(nblk, lambda h: h))],
        out_shape=[out_sd, out_sd],
        scratch_shapes=[pltpu.VMEM((dv, dk), F32), pltpu.VMEM((dv, dk), F32)],
        compiler_params=_params("parallel", "parallel", "arbitrary"),
        name="gla_scan",
    )(p_gla, p_gla, p_gla, lr, p_gla, p_gla, p_gla, lr,
      w2f, bf.reshape(1, -1), w2b, bb.reshape(1, -1), *consts)


def _hgrn_scan(p_tail, lb_fwd, lb_bwd, lmask, batch, seq):
    t = p_tail.shape[0]
    tb, c = SCAN_BLOCK, SCAN_CHUNK
    nblk = seq // tb
    dk, dv, nh = HGRN_K_DIM, HGRN_V_DIM, HGRN_HEADS
    consts = _scan_constants(c, False) + _scan_constants(c, True)
    n_layers = lb_fwd.shape[0]
    tok_specs = []
    for idx, z_col in ((_fwd_idx, nh), (_bwd_idx, 2 * nh)):
        tok_specs += [
            pl.BlockSpec((tb, dk), idx(nblk, lambda h: h)),
            pl.BlockSpec((tb, dk), idx(nblk, functools.partial(lambda h, z: z + h, z=z_col))),
            pl.BlockSpec((tb, dv), idx(nblk, lambda h: (3 * nh * dk) // dv + h)),
        ]
    lb_specs = [pl.BlockSpec((n_layers, dk), lambda b, h, c_: (0, h)),
                pl.BlockSpec((n_layers, dk), lambda b, h, c_: (0, h)),
                pl.BlockSpec((n_layers, 1), lambda b, h, c_: (0, 0))]
    out_sd = jax.ShapeDtypeStruct((t, nh * dv), F32)
    return pl.pallas_call(
        functools.partial(_hgrn_kernel, chunk=c),
        grid=(batch, nh, nblk),
        in_specs=tok_specs + lb_specs + [_const_spec(a) for a in consts],
        out_specs=[pl.BlockSpec((tb, dv), _fwd_idx(nblk, lambda h: h)),
                   pl.BlockSpec((tb, dv), _bwd_idx(nblk, lambda h: h))],
        out_shape=[out_sd, out_sd],
        scratch_shapes=[pltpu.VMEM((dv, dk), F32), pltpu.VMEM((dv, dk), F32)],
        compiler_params=_params("parallel", "parallel", "arbitrary"),
        name="hgrn_scan",
    )(p_tail, p_tail, p_tail, p_tail, p_tail, p_tail, lb_fwd, lb_bwd, lmask, *consts)


def _headnorm_gate_kernel(of_ref, ob_ref, r_ref, gain_ref, o_ref, *, hd):
    o = of_ref[...] + ob_ref[...]
    gain = gain_ref[...]
    outs = []
    for h in range(o.shape[1] // hd):
        blk = o[:, h * hd:(h + 1) * hd]
        ms = jnp.mean(blk * blk, axis=-1, keepdims=True)
        outs.append(blk * lax.rsqrt(ms + NORM_EPS) * gain)
    y = jnp.concatenate(outs, axis=-1) * _silu(r_ref[...])
    o_ref[...] = y.astype(o_ref.dtype)


def _headnorm_gate(o_f, o_b, p, r_off, gain, hd, tm=512):
    t, w = o_f.shape
    return pl.pallas_call(
        functools.partial(_headnorm_gate_kernel, hd=hd),
        grid=(t // tm,),
        in_specs=[pl.BlockSpec((tm, w), lambda i: (i, 0)),
                  pl.BlockSpec((tm, w), lambda i: (i, 0)),
                  pl.BlockSpec((tm, w), lambda i: (i, r_off)),
                  pl.BlockSpec((1, hd), lambda i: (0, 0))],
        out_specs=pl.BlockSpec((tm, w), lambda i: (i, 0)),
        out_shape=jax.ShapeDtypeStruct((t, w), BF16),
        compiler_params=_params("parallel"),
        name="headnorm_gate",
    )(o_f, o_b, p, gain.reshape(1, hd))


def _rotary_tables(seq):
    half = ATTN_HEAD_DIM // 2
    inv_freq = ROPE_THETA ** (-jnp.arange(half, dtype=F32) / half)
    ang = jnp.arange(seq, dtype=jnp.int32).astype(F32)[:, None] * inv_freq[None, :]
    cos, sin = jnp.cos(ang), jnp.sin(ang)
    return jnp.concatenate([cos, cos], axis=-1), jnp.concatenate([-sin, sin], axis=-1)


def kernel(x, ffn1_norm, ffn1_w_in, ffn1_w_out, mix_norm, w_in, attn_q_norm, attn_k_norm, attn_lambda, attn_sub_norm, gla_gate_w2_fwd, gla_gate_b_fwd, gla_gate_w2_bwd, gla_gate_b_bwd, gla_out_norm, hgrn_lb_fwd, hgrn_lb_bwd, hgrn_out_norm, w_branch_attn, w_branch_gla, w_branch_hgrn, w_out, ffn2_norm, ffn2_w_in, ffn2_w_out):
    batch, seq, d = x.shape
    t = batch * seq
    depth = w_in.shape[0]
    cosf, sinf = _rotary_tables(seq)

    qk_w = 2 * ATTN_HEADS * 2 * ATTN_HEAD_DIM
    av_w = ATTN_HEADS * ATTN_V_DIM
    gla_w = 2 * GLA_HEADS * GLA_K_DIM + 2 * GLA_HEADS * GLA_V_DIM
    lr_w = 2 * GLA_GATE_RANK
    hg_w = 3 * HGRN_HEADS * HGRN_K_DIM + 2 * HGRN_HEADS * HGRN_V_DIM
    c_av, c_gla = qk_w, qk_w + av_w
    c_lr = c_gla + gla_w
    c_tail = c_lr + lr_w
    tail_w = hg_w + 3 * d

    lam_inits = jnp.asarray([0.8 - 0.6 * math.exp(-0.3 * l) for l in range(depth)], F32)
    lmasks = jnp.asarray(np.tril(np.ones((depth, depth), np.float32)))[:, :, None]
    layer_ids = jnp.arange(depth, dtype=jnp.int32).reshape(depth, 1)

    w_lr = w_in[:, :, c_lr:c_tail]
    w_tail = w_in[:, :, c_tail:].astype(BF16)
    w_qk = w_in[:, :, :qk_w].astype(BF16)

    tm, tn = 1024, 512
    tm_qk, tn_qk = 512, 2048
    q_scale = ATTN_HEAD_DIM ** -0.5 * LOG2_E
    n_sub = qk_w // (2 * ATTN_HEAD_DIM)
    nrow = seq // tm_qk
    rot_spec = pl.BlockSpec((tm_qk, LANES), lambda j, i, l: (i % nrow, 0))

    def layer(xc, p):
        (lid, f1n, mn, aqn, akn, alam, asn, w2f, bf, w2b, bb, gon, hon, f2n, lam_init, lmask) = p

        xc = _ffn(lid, xc, _rmsnorm(xc, f1n), ffn1_w_in, ffn1_w_out)

        hn = _rmsnorm(xc, mn)
        qk_gain = jnp.concatenate([jnp.tile(aqn * q_scale, n_sub), jnp.tile(akn, n_sub)]).reshape(1, qk_w)
        qk = _matmul("proj_qk", lid, [hn], [(w_qk, 0)], (0,), _ep_qk_norm_rotary, qk_w, BF16, tm_qk, tn_qk,
                     extras=[(qk_gain, _row_spec(tn_qk)), (cosf, rot_spec), (sinf, rot_spec)])
        p_av = _matmul("proj_av", lid, [hn], [(w_in, c_av // tn)], (0,), _ep_plain, av_w, BF16, tm, tn)
        p_gla = _matmul("proj_gla", lid, [hn], [(w_in, c_gla // tn)], (0,), _ep_plain, gla_w, F32, tm, tn)
        p_lr = _matmul("proj_lr", lid, [hn], [(w_lr, 0)], (0,), _ep_plain, lr_w, F32, tm, lr_w)
        p_tail = _matmul("proj_tail", lid, [hn], [(w_tail, 0)], (0,), _ep_plain, tail_w, F32, tm, tn)

        o_a = _attention(qk, p_av, alam, asn, lam_init, batch, seq)

        og_f, og_b = _gla_scan(p_gla, p_lr, w2f, bf, w2b, bb, batch, seq)
        u_g = _headnorm_gate(og_f, og_b, p_gla, gla_w // (GLA_HEADS * GLA_V_DIM) - 1, gon, GLA_V_DIM)

        oh_f, oh_b = _hgrn_scan(p_tail, hgrn_lb_fwd, hgrn_lb_bwd, lmask, batch, seq)
        u_h = _headnorm_gate(oh_f, oh_b, p_tail, hg_w // (HGRN_HEADS * HGRN_V_DIM) - 1, hon, HGRN_V_DIM)

        g0 = hg_w // tn
        merged = _matmul(
            "merge", lid, [o_a, u_g, u_h],
            [(w_branch_attn, 0), (w_branch_gla, 0), (w_branch_hgrn, 0)], (0, 1, 2),
            _ep_gated_sum, d, BF16, tm // 2, tn,
            extras=[(p_tail, _tile_spec(tm // 2, tn, g0)), (p_tail, _tile_spec(tm // 2, tn, g0 + d // tn)),
                    (p_tail, _tile_spec(tm // 2, tn, g0 + 2 * d // tn))])
        xc = _matmul("out_proj", lid, [merged], [(w_out, 0)], (0,), _ep_residual, d, F32, tm, tn,
                     extras=[(xc, _tile_spec(tm, tn))])

        xc = _ffn(lid, xc, _rmsnorm(xc, f2n), ffn2_w_in, ffn2_w_out)
        return xc, None

    per_layer = (layer_ids, ffn1_norm, mix_norm, attn_q_norm, attn_k_norm,
                 attn_lambda, attn_sub_norm, gla_gate_w2_fwd, gla_gate_b_fwd, gla_gate_w2_bwd,
                 gla_gate_b_bwd, gla_out_norm, hgrn_out_norm, ffn2_norm, lam_inits, lmasks)
    out, _ = lax.scan(layer, x.reshape(t, d), per_layer)
    return out.reshape(batch, seq, d)
```

```python
import functools
import math

import numpy as np
import jax
import jax.numpy as jnp
from jax import lax
from jax.experimental import pallas as pl
from jax.experimental.pallas import tpu as pltpu

F32 = jnp.float32
BF16 = jnp.bfloat16

NORM_EPS = 1e-6
LB_FLOOR = 1e-20
ROPE_THETA = 10000.0
GLA_GATE_NORMALIZER = 16.0

ATTN_HEADS = 8
ATTN_HEAD_DIM = 128
ATTN_V_DIM = 256
GLA_HEADS = 4
GLA_K_DIM = 128
GLA_V_DIM = 256
GLA_GATE_RANK = 16
HGRN_HEADS = 8
HGRN_K_DIM = 128
HGRN_V_DIM = 128

LANES = 128
VMEM_LIMIT_BYTES = 56 * 2**20
SCAN_CHUNK = 128
SCAN_BLOCK = 512
LOG2_E = math.log2(math.e)


def _params(*sem):
    return pltpu.CompilerParams(dimension_semantics=sem, vmem_limit_bytes=VMEM_LIMIT_BYTES)


def _dot(a, b):
    return jnp.dot(a, b, preferred_element_type=F32)


def _dot_nt(a, b):
    return lax.dot_general(a, b, (((1,), (1,)), ((), ())), preferred_element_type=F32)


def _dot_tn(a, b):
    return lax.dot_general(a, b, (((0,), (0,)), ((), ())), preferred_element_type=F32)


def _sigmoid(x):
    return 1.0 / (1.0 + jnp.exp(-x))


def _silu(x):
    return x * _sigmoid(x)


def _log_sigmoid(x):
    return jnp.minimum(x, 0.0) - jnp.log1p(jnp.exp(-jnp.abs(x)))


def _split3(x):
    hi = x.astype(BF16)
    r1 = x - hi.astype(F32)
    mid = r1.astype(BF16)
    lo = (r1 - mid.astype(F32)).astype(BF16)
    return hi, mid, lo


def _rmsnorm_kernel(x_ref, g_ref, o_ref):
    x = x_ref[...]
    ms = jnp.mean(x * x, axis=-1, keepdims=True)
    o_ref[...] = (x * lax.rsqrt(ms + NORM_EPS) * g_ref[...]).astype(o_ref.dtype)


def _rmsnorm(x, gain, tm=256):
    t, d = x.shape
    return pl.pallas_call(
        _rmsnorm_kernel,
        grid=(t // tm,),
        in_specs=[pl.BlockSpec((tm, d), lambda i: (i, 0)),
                  pl.BlockSpec((1, d), lambda i: (0, 0))],
        out_specs=pl.BlockSpec((tm, d), lambda i: (i, 0)),
        out_shape=jax.ShapeDtypeStruct((t, d), BF16),
        compiler_params=_params("parallel"),
        name="rmsnorm",
    )(x, gain.reshape(1, d))


def _mm_kernel(*refs, n_a, pairs, needs_cast, n_extra, epilogue):
    n_w = len(pairs)
    a_refs = refs[:n_a]
    w_refs = refs[n_a:n_a + n_w]
    e_refs = refs[n_a + n_w:n_a + n_w + n_extra]
    o_ref = refs[n_a + n_w + n_extra]
    scratch = refs[n_a + n_w + n_extra + 1:]
    w_bf = []
    si = 0
    for k in range(n_w):
        if needs_cast[k]:
            sc = scratch[si]
            si += 1

            @pl.when(pl.program_id(1) == 0)
            def _(sc=sc, w_ref=w_refs[k]):
                sc[...] = w_ref[...].astype(BF16)

            w_bf.append(sc)
        else:
            w_bf.append(w_refs[k])
    accs = [_dot(a_refs[pairs[k]][...], w_bf[k][...]) for k in range(n_w)]
    o_ref[...] = epilogue(accs, [e[...] for e in e_refs]).astype(o_ref.dtype)


def _matmul(name, layer, a_list, w_list, pairs, epilogue, n_out, out_dtype, tm, tn, extras=()):
    t = a_list[0].shape[0]
    in_specs, args, needs_cast, scratch = [], [], [], []
    for a in a_list:
        in_specs.append(pl.BlockSpec((tm, a.shape[1]), lambda j, i: (i, 0)))
        args.append(a)
    for w, off in w_list:
        k_dim = w.shape[-2]
        if w.ndim == 3:
            in_specs.append(pl.BlockSpec((None, k_dim, tn),
                                         functools.partial(lambda j, i, o: (layer, 0, j + o), o=off)))
        else:
            in_specs.append(pl.BlockSpec((k_dim, tn), functools.partial(lambda j, i, o: (0, j + o), o=off)))
        if w.dtype != BF16:
            scratch.append(pltpu.VMEM((k_dim, tn), BF16))
        needs_cast.append(w.dtype != BF16)
        args.append(w)
    for e, spec in extras:
        in_specs.append(spec)
        args.append(e)
    return pl.pallas_call(
        functools.partial(_mm_kernel, n_a=len(a_list), pairs=tuple(pairs), needs_cast=tuple(needs_cast),
                          n_extra=len(extras), epilogue=epilogue),
        grid=(n_out // tn, t // tm),
        in_specs=in_specs,
        out_specs=pl.BlockSpec((tm, tn), lambda j, i: (i, j)),
        out_shape=jax.ShapeDtypeStruct((t, n_out), out_dtype),
        scratch_shapes=scratch,
        compiler_params=_params("parallel", "arbitrary"),
        name=name,
    )(*args)


def _tile_spec(tm, tn, off=0):
    return pl.BlockSpec((tm, tn), functools.partial(lambda j, i, o: (i, j + o), o=off))


def _row_spec(tn):
    return pl.BlockSpec((1, tn), lambda j, i: (0, j))


def _ep_plain(accs, extras):
    return accs[0]


def _ep_swiglu(accs, extras):
    return _silu(accs[0]) * accs[1]


def _ep_residual_half(accs, extras):
    return extras[0] + 0.5 * accs[0]


def _ep_residual(accs, extras):
    return extras[0] + accs[0]


def _ep_gated_sum(accs, extras):
    gates = [_sigmoid(e.astype(F32)) for e in extras]
    return gates[0] * accs[0] + gates[1] * accs[1] + gates[2] * accs[2]


def _ep_qk_norm_rotary(accs, extras):
    acc = accs[0]
    gain, cosf, sinf = extras
    outs = []
    for c in range(acc.shape[1] // LANES):
        blk = acc[:, c * LANES:(c + 1) * LANES]
        ms = jnp.mean(blk * blk, axis=-1, keepdims=True)
        y = blk * lax.rsqrt(ms + NORM_EPS) * gain[:, c * LANES:(c + 1) * LANES]
        outs.append(y * cosf + pltpu.roll(y, LANES // 2, 1) * sinf)
    return jnp.concatenate(outs, axis=-1)


def _ffn(layer, x, xn, w_in, w_out):
    f, d = w_out.shape[1], w_out.shape[2]
    tm_in, tf = 1024, 256
    tm_out, tn = 512, 1024
    h = _matmul("ffn_in", layer, [xn], [(w_in, 0), (w_in, f // tf)], (0, 0), _ep_swiglu,
                f, BF16, tm_in, tf)
    return _matmul("ffn_out", layer, [h], [(w_out, 0)], (0,), _ep_residual_half,
                   d, F32, tm_out, tn, extras=[(x, _tile_spec(tm_out, tn))])


def _attn_kernel(q_ref, k_ref, v_ref, lam_ref, gain_ref, linit_ref, o_ref):
    dh = ATTN_HEAD_DIM
    v = v_ref[...]
    lv = lam_ref[...]
    lam_init = linit_ref[...]
    s01 = jnp.sum(lv[0:1] * lv[1:2], axis=-1, keepdims=True)
    s23 = jnp.sum(lv[2:3] * lv[3:4], axis=-1, keepdims=True)
    lam = jnp.exp(s01) - jnp.exp(s23) + lam_init

    def one_map(m):
        s = _dot_nt(q_ref[:, m * dh:(m + 1) * dh], k_ref[:, m * dh:(m + 1) * dh])
        p = jnp.exp2(s - jnp.max(s, axis=-1, keepdims=True))
        inv_l = 1.0 / jnp.sum(p, axis=-1, keepdims=True)
        return _dot(p.astype(BF16), v), inv_l

    o1, r1 = one_map(0)
    o2, r2 = one_map(1)
    o = o1 * r1 - o2 * (lam * r2)
    ms = jnp.mean(o * o, axis=-1, keepdims=True)
    o = o * lax.rsqrt(ms + NORM_EPS) * gain_ref[...] * (1.0 - lam_init)
    o_ref[...] = o.astype(o_ref.dtype)


def _attention(qk, v, lam_vec, sub_gain, lam_init, batch, seq, tq=512):
    t = qk.shape[0]
    hd = 2 * ATTN_HEAD_DIM
    nq = seq // tq
    return pl.pallas_call(
        _attn_kernel,
        grid=(batch, ATTN_HEADS, nq),
        in_specs=[
            pl.BlockSpec((tq, hd), lambda b, h, i: (b * nq + i, h)),
            pl.BlockSpec((seq, hd), lambda b, h, i: (b, ATTN_HEADS + h)),
            pl.BlockSpec((seq, ATTN_V_DIM), lambda b, h, i: (b, h)),
            pl.BlockSpec((4, ATTN_HEAD_DIM), lambda b, h, i: (0, 0)),
            pl.BlockSpec((1, ATTN_V_DIM), lambda b, h, i: (0, 0)),
            pl.BlockSpec((1, 1), lambda b, h, i: (0, 0)),
        ],
        out_specs=pl.BlockSpec((tq, ATTN_V_DIM), lambda b, h, i: (b * nq + i, h)),
        out_shape=jax.ShapeDtypeStruct((t, ATTN_HEADS * ATTN_V_DIM), BF16),
        compiler_params=_params("parallel", "parallel", "arbitrary"),
        name="diff_attention",
    )(qk, qk, v, lam_vec, sub_gain.reshape(1, -1), lam_init.reshape(1, 1))


def _scan_constants(c, rev):
    idx = np.arange(c)
    if rev:
        cum = idx[None, :] >= idx[:, None]
    else:
        cum = idx[None, :] <= idx[:, None]
    valids = [np.eye(c, dtype=np.float32)]
    s = c // 2
    while s >= 1:
        blk = idx // (2 * s)
        late = (idx // s) % 2
        same = blk[:, None] == blk[None, :]
        if not rev:
            valid = same & (late[:, None] == 1) & (late[None, :] == 0)
        else:
            valid = same & (late[:, None] == 0) & (late[None, :] == 1)
        valids.append(valid.astype(np.float32))
        s //= 2
    return (jnp.asarray(cum.astype(np.float32), dtype=BF16),
            jnp.asarray(np.stack(valids), dtype=F32))


def _level_masks(c, dk, n_lvl, rev):
    row = lax.broadcasted_iota(jnp.int32, (c, dk), 0)
    masks = []
    for lvl in range(1, n_lvl + 1):
        late = ((row >> (n_lvl - lvl)) & 1) == 1
        masks.append(jnp.logical_not(late) if rev else late)
    return masks


def _mid_row(cum, s):
    c, dk = cum.shape
    if s >= 8:
        parts = [jnp.broadcast_to(cum[b * 2 * s + s:b * 2 * s + s + 1, :], (2 * s, dk))
                 for b in range(c // (2 * s))]
        return parts[0] if len(parts) == 1 else jnp.concatenate(parts, axis=0)
    x3 = cum.reshape(c // 8, 8, dk)
    sub = lax.broadcasted_iota(jnp.int32, x3.shape, 1)

    def row(r):
        return jnp.broadcast_to(x3[:, r:r + 1, :], x3.shape)

    if s == 4:
        r3 = row(4)
    elif s == 2:
        r3 = jnp.where(sub < 4, row(2), row(6))
    else:
        r3 = jnp.where(sub < 2, row(1), jnp.where(sub < 4, row(3), jnp.where(sub < 6, row(5), row(7))))
    return r3.reshape(c, dk)


def _scan_chunk(q, k, v, g, st, cum_ref, valid_ref, masks, rev):
    c = q.shape[0]
    n_lvl = len(masks)
    tri = cum_ref[...]
    gh, gm, gl = _split3(g)
    cum = _dot(tri, gh) + _dot(tri, gm) + _dot(tri, gl)
    total = cum[0:1] if rev else cum[c - 1:c]
    a = valid_ref[0] * _dot_nt(q.astype(BF16), k.astype(BF16))
    for lvl in range(1, n_lvl + 1):
        e = -jnp.abs(cum - _mid_row(cum, c >> lvl))
        x = (jnp.where(masks[lvl - 1], q, k) * jnp.exp2(e)).astype(BF16)
        a = a + valid_ref[lvl] * _dot_nt(x, x)
    qd = (q * jnp.exp2(cum)).astype(BF16)
    kd = (k * jnp.exp2(total - cum)).astype(BF16)
    vb = v.astype(BF16)
    o = _dot(a.astype(BF16), vb) + _dot_nt(qd, st.astype(BF16))
    return o, st * jnp.exp2(total) + _dot_tn(vb, kd)


def _gla_gate(lr, w2, bias):
    lr_hi = lr.astype(BF16)
    lr_lo = (lr - lr_hi.astype(F32)).astype(BF16)
    w2_hi = w2.astype(BF16)
    w2_lo = (w2 - w2_hi.astype(F32)).astype(BF16)
    z = _dot(lr_hi, w2_hi) + _dot(lr_hi, w2_lo) + _dot(lr_lo, w2_hi) + bias
    return _log_sigmoid(z) * (LOG2_E / GLA_GATE_NORMALIZER)


def _gla_kernel(qf_ref, kf_ref, vf_ref, lrf_ref, qb_ref, kb_ref, vb_ref, lrb_ref,
                w2f_ref, bf_ref, w2b_ref, bb_ref, dallf_ref, validf_ref, dallb_ref, validb_ref,
                of_ref, ob_ref, stf_ref, stb_ref, *, chunk):
    @pl.when(pl.program_id(2) == 0)
    def _():
        stf_ref[...] = jnp.zeros_like(stf_ref)
        stb_ref[...] = jnp.zeros_like(stb_ref)

    r = GLA_GATE_RANK
    gf_all = _gla_gate(lrf_ref[...][:, 0:r], w2f_ref[...], bf_ref[...])
    gb_all = _gla_gate(lrb_ref[...][:, r:2 * r], w2b_ref[...], bb_ref[...])
    n_lvl = validf_ref.shape[0] - 1
    mf = _level_masks(chunk, GLA_K_DIM, n_lvl, False)
    mb = _level_masks(chunk, GLA_K_DIM, n_lvl, True)
    scale = GLA_K_DIM ** -0.5
    n = qf_ref.shape[0] // chunk
    stf, stb = stf_ref[...], stb_ref[...]
    for ci in range(n):
        sf = slice(ci * chunk, (ci + 1) * chunk)
        sb = slice((n - 1 - ci) * chunk, (n - ci) * chunk)
        o, stf = _scan_chunk(qf_ref[sf, :] * scale, kf_ref[sf, :], vf_ref[sf, :], gf_all[sf, :],
                             stf, dallf_ref, validf_ref, mf, False)
        of_ref[sf, :] = o
        o, stb = _scan_chunk(qb_ref[sb, :] * scale, kb_ref[sb, :], vb_ref[sb, :], gb_all[sb, :],
                             stb, dallb_ref, validb_ref, mb, True)
        ob_ref[sb, :] = o
    stf_ref[...] = stf
    stb_ref[...] = stb


def _hgrn_lower_bound(logits, lmask):
    pe = jnp.exp(logits - jnp.max(logits, axis=0, keepdims=True))
    p = pe / jnp.sum(pe, axis=0, keepdims=True)
    lb = jnp.sum(p * lmask, axis=0, keepdims=True) - p[0:1]
    lb = jnp.clip(lb, 0.0, 1.0 - 1e-6)
    return lb, jnp.log(jnp.maximum(lb, LB_FLOOR)), jnp.log1p(-lb)


def _hgrn_gate(z, lb, log_lb, log_1m):
    t2 = log_1m + _log_sigmoid(z)
    g = jnp.maximum(log_lb, t2) + jnp.log1p(jnp.exp(-jnp.abs(log_lb - t2)))
    return (1.0 - lb) * _sigmoid(-z), g * LOG2_E


def _hgrn_kernel(qf_ref, zf_ref, vf_ref, qb_ref, zb_ref, vb_ref, lbf_ref, lbb_ref, lmask_ref,
                 dallf_ref, validf_ref, dallb_ref, validb_ref, of_ref, ob_ref, stf_ref, stb_ref,
                 *, chunk):
    @pl.when(pl.program_id(2) == 0)
    def _():
        stf_ref[...] = jnp.zeros_like(stf_ref)
        stb_ref[...] = jnp.zeros_like(stb_ref)

    lbf = _hgrn_lower_bound(lbf_ref[...], lmask_ref[...])
    lbb = _hgrn_lower_bound(lbb_ref[...], lmask_ref[...])
    n_lvl = validf_ref.shape[0] - 1
    mf = _level_masks(chunk, HGRN_K_DIM, n_lvl, False)
    mb = _level_masks(chunk, HGRN_K_DIM, n_lvl, True)
    scale = HGRN_K_DIM ** -0.5
    n = qf_ref.shape[0] // chunk
    stf, stb = stf_ref[...], stb_ref[...]
    for ci in range(n):
        sf = slice(ci * chunk, (ci + 1) * chunk)
        sb = slice((n - 1 - ci) * chunk, (n - ci) * chunk)
        k, g = _hgrn_gate(zf_ref[sf, :], *lbf)
        o, stf = _scan_chunk(qf_ref[sf, :] * scale, k, vf_ref[sf, :], g,
                             stf, dallf_ref, validf_ref, mf, False)
        of_ref[sf, :] = o
        k, g = _hgrn_gate(zb_ref[sb, :], *lbb)
        o, stb = _scan_chunk(qb_ref[sb, :] * scale, k, vb_ref[sb, :], g,
                             stb, dallb_ref, validb_ref, mb, True)
        ob_ref[sb, :] = o
    stf_ref[...] = stf
    stb_ref[...] = stb


def _fwd_idx(nblk, col):
    return lambda b, h, c: (b * nblk + c, col(h))


def _bwd_idx(nblk, col):
    return lambda b, h, c: (b * nblk + (nblk - 1 - c), col(h))


def _const_spec(arr):
    zeros = (0,) * arr.ndim
    return pl.BlockSpec(arr.shape, lambda b, h, c: zeros)


def _gla_scan(p_gla, lr, w2f, bf, w2b, bb, batch, seq):
    t = p_gla.shape[0]
    tb, c = SCAN_BLOCK, SCAN_CHUNK
    nblk = seq // tb
    dk, dv, nh = GLA_K_DIM, GLA_V_DIM, GLA_HEADS
    consts = _scan_constants(c, False) + _scan_constants(c, True)
    tok_specs = []
    for idx in (_fwd_idx, _bwd_idx):
        tok_specs += [
            pl.BlockSpec((tb, dk), idx(nblk, lambda h: h)),
            pl.BlockSpec((tb, dk), idx(nblk, lambda h: nh + h)),
            pl.BlockSpec((tb, dv), idx(nblk, lambda h: (2 * nh * dk) // dv + h)),
            pl.BlockSpec((tb, lr.shape[1]), idx(nblk, lambda h: 0)),
        ]
    gate_specs = [pl.BlockSpec((GLA_GATE_RANK, dk), lambda b, h, c_: (0, h)),
                  pl.BlockSpec((1, dk), lambda b, h, c_: (0, h))] * 2
    out_sd = jax.ShapeDtypeStruct((t, nh * dv), F32)
    return pl.pallas_call(
        functools.partial(_gla_kernel, chunk=c),
        grid=(batch, nh, nblk),
        in_specs=tok_specs + gate_specs + [_const_spec(a) for a in consts],
        out_specs=[pl.BlockSpec((tb, dv), _fwd_idx(nblk, lambda h: h)),
                   pl.BlockSpec((tb, dv), _bwd_idx(nblk, lambda h: h))],
        out_shape=[out_sd, out_sd],
        scratch_shapes=[pltpu.VMEM((dv, dk), F32), pltpu.VMEM((dv, dk), F32)],
        compiler_params=_params("parallel", "parallel", "arbitrary"),
        name="gla_scan",
    )(p_gla, p_gla, p_gla, lr, p_gla, p_gla, p_gla, lr,
      w2f, bf.reshape(1, -1), w2b, bb.reshape(1, -1), *consts)


def _hgrn_scan(p_hg, lb_fwd, lb_bwd, lmask, batch, seq):
    t = p_hg.shape[0]
    tb, c = SCAN_BLOCK, SCAN_CHUNK
    nblk = seq // tb
    dk, dv, nh = HGRN_K_DIM, HGRN_V_DIM, HGRN_HEADS
    consts = _scan_constants(c, False) + _scan_constants(c, True)
    n_layers = lb_fwd.shape[0]
    tok_specs = []
    for idx, z_col in ((_fwd_idx, nh), (_bwd_idx, 2 * nh)):
        tok_specs += [
            pl.BlockSpec((tb, dk), idx(nblk, lambda h: h)),
            pl.BlockSpec((tb, dk), idx(nblk, functools.partial(lambda h, z: z + h, z=z_col))),
            pl.BlockSpec((tb, dv), idx(nblk, lambda h: (3 * nh * dk) // dv + h)),
        ]
    lb_specs = [pl.BlockSpec((n_layers, dk), lambda b, h, c_: (0, h)),
                pl.BlockSpec((n_layers, dk), lambda b, h, c_: (0, h)),
                pl.BlockSpec((n_layers, 1), lambda b, h, c_: (0, 0))]
    out_sd = jax.ShapeDtypeStruct((t, nh * dv), F32)
    return pl.pallas_call(
        functools.partial(_hgrn_kernel, chunk=c),
        grid=(batch, nh, nblk),
        in_specs=tok_specs + lb_specs + [_const_spec(a) for a in consts],
        out_specs=[pl.BlockSpec((tb, dv), _fwd_idx(nblk, lambda h: h)),
                   pl.BlockSpec((tb, dv), _bwd_idx(nblk, lambda h: h))],
        out_shape=[out_sd, out_sd],
        scratch_shapes=[pltpu.VMEM((dv, dk), F32), pltpu.VMEM((dv, dk), F32)],
        compiler_params=_params("parallel", "parallel", "arbitrary"),
        name="hgrn_scan",
    )(p_hg, p_hg, p_hg, p_hg, p_hg, p_hg, lb_fwd, lb_bwd, lmask, *consts)


def _headnorm_gate_kernel(of_ref, ob_ref, r_ref, gain_ref, o_ref, *, hd):
    o = of_ref[...] + ob_ref[...]
    gain = gain_ref[...]
    outs = []
    for h in range(o.shape[1] // hd):
        blk = o[:, h * hd:(h + 1) * hd]
        ms = jnp.mean(blk * blk, axis=-1, keepdims=True)
        outs.append(blk * lax.rsqrt(ms + NORM_EPS) * gain)
    y = jnp.concatenate(outs, axis=-1) * _silu(r_ref[...])
    o_ref[...] = y.astype(o_ref.dtype)


def _headnorm_gate(o_f, o_b, p, r_off, gain, hd, tm=512):
    t, w = o_f.shape
    return pl.pallas_call(
        functools.partial(_headnorm_gate_kernel, hd=hd),
        grid=(t // tm,),
        in_specs=[pl.BlockSpec((tm, w), lambda i: (i, 0)),
                  pl.BlockSpec((tm, w), lambda i: (i, 0)),
                  pl.BlockSpec((tm, w), lambda i: (i, r_off)),
                  pl.BlockSpec((1, hd), lambda i: (0, 0))],
        out_specs=pl.BlockSpec((tm, w), lambda i: (i, 0)),
        out_shape=jax.ShapeDtypeStruct((t, w), BF16),
        compiler_params=_params("parallel"),
        name="headnorm_gate",
    )(o_f, o_b, p, gain.reshape(1, hd))


def _rotary_tables(seq):
    half = ATTN_HEAD_DIM // 2
    inv_freq = ROPE_THETA ** (-jnp.arange(half, dtype=F32) / half)
    ang = jnp.arange(seq, dtype=jnp.int32).astype(F32)[:, None] * inv_freq[None, :]
    cos, sin = jnp.cos(ang), jnp.sin(ang)
    return jnp.concatenate([cos, cos], axis=-1), jnp.concatenate([-sin, sin], axis=-1)


def kernel(x, ffn1_norm, ffn1_w_in, ffn1_w_out, mix_norm, w_in, attn_q_norm, attn_k_norm, attn_lambda, attn_sub_norm, gla_gate_w2_fwd, gla_gate_b_fwd, gla_gate_w2_bwd, gla_gate_b_bwd, gla_out_norm, hgrn_lb_fwd, hgrn_lb_bwd, hgrn_out_norm, w_branch_attn, w_branch_gla, w_branch_hgrn, w_out, ffn2_norm, ffn2_w_in, ffn2_w_out):
    batch, seq, d = x.shape
    t = batch * seq
    depth = w_in.shape[0]
    cosf, sinf = _rotary_tables(seq)

    qk_w = 2 * ATTN_HEADS * 2 * ATTN_HEAD_DIM
    av_w = ATTN_HEADS * ATTN_V_DIM
    gla_w = 2 * GLA_HEADS * GLA_K_DIM + 2 * GLA_HEADS * GLA_V_DIM
    lr_w = 2 * GLA_GATE_RANK
    hg_w = 3 * HGRN_HEADS * HGRN_K_DIM + 2 * HGRN_HEADS * HGRN_V_DIM
    c_av, c_gla = qk_w, qk_w + av_w
    c_lr = c_gla + gla_w
    c_tail = c_lr + lr_w

    lmasks = np.tril(np.ones((depth, depth), np.float32))[:, :, None]

    tm, tn = 1024, 512
    tm_w, tn_w = 512, 2048
    q_scale = ATTN_HEAD_DIM ** -0.5 * LOG2_E
    n_sub = qk_w // (2 * ATTN_HEAD_DIM)
    nrow = seq // tm_w
    rot_spec = pl.BlockSpec((tm_w, LANES), lambda j, i: (i % nrow, 0))

    xc = x.reshape(t, d)
    for l in range(depth):
        lam_init = jnp.full((1, 1), 0.8 - 0.6 * math.exp(-0.3 * l), F32)
        xc = _ffn(l, xc, _rmsnorm(xc, ffn1_norm[l]), ffn1_w_in, ffn1_w_out)

        hn = _rmsnorm(xc, mix_norm[l])
        w_qk = w_in[l, :, :qk_w].astype(BF16)
        w_hg = w_in[l, :, c_tail:c_tail + hg_w].astype(BF16)
        w_gate = w_in[l, :, c_tail + hg_w:].astype(BF16)
        qk_gain = jnp.concatenate([jnp.tile(attn_q_norm[l] * q_scale, n_sub),
                                   jnp.tile(attn_k_norm[l], n_sub)]).reshape(1, qk_w)
        qk = _matmul("proj_qk", l, [hn], [(w_qk, 0)], (0,), _ep_qk_norm_rotary, qk_w, BF16, tm_w, tn_w,
                     extras=[(qk_gain, _row_spec(tn_w)), (cosf, rot_spec), (sinf, rot_spec)])
        p_av = _matmul("proj_av", l, [hn], [(w_in, c_av // tn)], (0,), _ep_plain, av_w, BF16, tm, tn)
        p_gla = _matmul("proj_gla", l, [hn], [(w_in, c_gla // tn)], (0,), _ep_plain, gla_w, F32, tm, tn)
        p_lr = _matmul("proj_lr", l, [hn], [(w_in, c_lr // LANES)], (0,), _ep_plain, LANES, F32, tm, LANES)
        p_hg = _matmul("proj_hgrn", l, [hn], [(w_hg, 0)], (0,), _ep_plain, hg_w, F32, tm_w, tn_w // 2)
        p_gate = _matmul("proj_gate", l, [hn], [(w_gate, 0)], (0,), _ep_plain, 3 * d, BF16, tm_w, tn_w)

        o_a = _attention(qk, p_av, attn_lambda[l], attn_sub_norm[l], lam_init, batch, seq)

        og_f, og_b = _gla_scan(p_gla, p_lr, gla_gate_w2_fwd[l], gla_gate_b_fwd[l],
                               gla_gate_w2_bwd[l], gla_gate_b_bwd[l], batch, seq)
        u_g = _headnorm_gate(og_f, og_b, p_gla, gla_w // (GLA_HEADS * GLA_V_DIM) - 1, gla_out_norm[l],
                             GLA_V_DIM)

        oh_f, oh_b = _hgrn_scan(p_hg, hgrn_lb_fwd, hgrn_lb_bwd, jnp.asarray(lmasks[l]), batch, seq)
        u_h = _headnorm_gate(oh_f, oh_b, p_hg, hg_w // (HGRN_HEADS * HGRN_V_DIM) - 1, hgrn_out_norm[l],
                             HGRN_V_DIM)

        merged = _matmul(
            "merge", l, [o_a, u_g, u_h],
            [(w_branch_attn, 0), (w_branch_gla, 0), (w_branch_hgrn, 0)], (0, 1, 2),
            _ep_gated_sum, d, BF16, tm // 2, tn,
            extras=[(p_gate, _tile_spec(tm // 2, tn, b * d // tn)) for b in range(3)])
        xc = _matmul("out_proj", l, [merged], [(w_out, 0)], (0,), _ep_residual, d, F32, tm, tn,
                     extras=[(xc, _tile_spec(tm, tn))])

        xc = _ffn(l, xc, _rmsnorm(xc, ffn2_norm[l]), ffn2_w_in, ffn2_w_out)
    return xc.reshape(batch, seq, d)
```

```python
import functools
import math

import numpy as np
import jax
import jax.numpy as jnp
from jax import lax
from jax.experimental import pallas as pl
from jax.experimental.pallas import tpu as pltpu

F32 = jnp.float32
BF16 = jnp.bfloat16

NORM_EPS = 1e-6
LB_FLOOR = 1e-20
ROPE_THETA = 10000.0
GLA_GATE_NORMALIZER = 16.0

ATTN_HEADS = 8
ATTN_HEAD_DIM = 128
ATTN_V_DIM = 256
GLA_HEADS = 4
GLA_K_DIM = 128
GLA_V_DIM = 256
GLA_GATE_RANK = 16
HGRN_HEADS = 8
HGRN_K_DIM = 128
HGRN_V_DIM = 128

LANES = 128
VMEM_LIMIT_BYTES = 56 * 2**20
SCAN_CHUNK = 128
SCAN_BLOCK = 512
LOG2_E = math.log2(math.e)


def _params(*sem):
    return pltpu.CompilerParams(dimension_semantics=sem, vmem_limit_bytes=VMEM_LIMIT_BYTES)


def _dot(a, b):
    return jnp.dot(a, b, preferred_element_type=F32)


def _dot_nt(a, b):
    return lax.dot_general(a, b, (((1,), (1,)), ((), ())), preferred_element_type=F32)


def _dot_tn(a, b):
    return lax.dot_general(a, b, (((0,), (0,)), ((), ())), preferred_element_type=F32)


def _sigmoid(x):
    return 1.0 / (1.0 + jnp.exp(-x))


def _silu(x):
    return x * _sigmoid(x)


def _log_sigmoid(x):
    return jnp.minimum(x, 0.0) - jnp.log1p(jnp.exp(-jnp.abs(x)))


def _split3(x):
    hi = x.astype(BF16)
    r1 = x - hi.astype(F32)
    mid = r1.astype(BF16)
    lo = (r1 - mid.astype(F32)).astype(BF16)
    return hi, mid, lo


def _rmsnorm_kernel(x_ref, g_ref, o_ref):
    x = x_ref[...]
    ms = jnp.mean(x * x, axis=-1, keepdims=True)
    o_ref[...] = (x * lax.rsqrt(ms + NORM_EPS) * g_ref[...]).astype(o_ref.dtype)


def _rmsnorm(x, gain, tm=256):
    t, d = x.shape
    return pl.pallas_call(
        _rmsnorm_kernel,
        grid=(t // tm,),
        in_specs=[pl.BlockSpec((tm, d), lambda i: (i, 0)),
                  pl.BlockSpec((1, d), lambda i: (0, 0))],
        out_specs=pl.BlockSpec((tm, d), lambda i: (i, 0)),
        out_shape=jax.ShapeDtypeStruct((t, d), BF16),
        compiler_params=_params("parallel"),
        name="rmsnorm",
    )(x, gain.reshape(1, d))


def _mm_kernel(*refs, n_a, pairs, needs_cast, n_extra, transposed, epilogue):
    n_w = len(pairs)
    a_refs = refs[:n_a]
    w_refs = refs[n_a:n_a + n_w]
    e_refs = refs[n_a + n_w:n_a + n_w + n_extra]
    o_ref = refs[n_a + n_w + n_extra]
    scratch = refs[n_a + n_w + n_extra + 1:]
    w_bf = []
    si = 0
    for k in range(n_w):
        if needs_cast[k]:
            sc = scratch[si]
            si += 1

            @pl.when(pl.program_id(1) == 0)
            def _(sc=sc, w_ref=w_refs[k]):
                sc[...] = w_ref[...].reshape(sc.shape).astype(BF16)

            w_bf.append(sc)
        else:
            w_bf.append(w_refs[k])
    dot = _dot_nt if transposed else _dot
    accs = [dot(a_refs[pairs[k]][...], w_bf[k][...].reshape(w_bf[k].shape[-2:])) for k in range(n_w)]
    o_ref[...] = epilogue(accs, [e[...] for e in e_refs]).astype(o_ref.dtype)


def _matmul(name, layer, a_list, w_list, pairs, epilogue, n_out, out_dtype, tm, tn, extras=(),
            transposed=False):
    t = a_list[0].shape[0]
    in_specs, args, needs_cast, scratch = [], [], [], []
    for a in a_list:
        in_specs.append(pl.BlockSpec((tm, a.shape[1]), lambda j, i: (i, 0)))
        args.append(a)
    for w, off in w_list:
        at = (layer,) if w.ndim == 3 else ()
        if transposed:
            blk = (tn, w.shape[-1])
            spec = pl.BlockSpec(tuple(pl.Element(n) for n in (1,) * len(at) + blk),
                                functools.partial(lambda j, i, o, at: at + (pl.multiple_of(o + j * tn, 32), 0),
                                                  o=off, at=at))
        else:
            blk = (w.shape[-2], tn)
            spec = pl.BlockSpec((None,) * len(at) + blk,
                                functools.partial(lambda j, i, o, at: at + (0, j + o), o=off, at=at))
        in_specs.append(spec)
        if w.dtype != BF16:
            scratch.append(pltpu.VMEM(blk, BF16))
        needs_cast.append(w.dtype != BF16)
        args.append(w)
    for e, spec in extras:
        in_specs.append(spec)
        args.append(e)
    return pl.pallas_call(
        functools.partial(_mm_kernel, n_a=len(a_list), pairs=tuple(pairs), needs_cast=tuple(needs_cast),
                          n_extra=len(extras), transposed=transposed, epilogue=epilogue),
        grid=(n_out // tn, t // tm),
        in_specs=in_specs,
        out_specs=pl.BlockSpec((tm, tn), lambda j, i: (i, j)),
        out_shape=jax.ShapeDtypeStruct((t, n_out), out_dtype),
        scratch_shapes=scratch,
        compiler_params=_params("parallel", "arbitrary"),
        name=name,
    )(*args)


def _tile_spec(tm, tn, off=0):
    return pl.BlockSpec((tm, tn), functools.partial(lambda j, i, o: (i, j + o), o=off))


def _row_spec(tn):
    return pl.BlockSpec((1, tn), lambda j, i: (0, j))


def _ep_plain(accs, extras):
    return accs[0]


def _ep_swiglu(accs, extras):
    return _silu(accs[0]) * accs[1]


def _ep_residual_half(accs, extras):
    return extras[0] + 0.5 * accs[0]


def _ep_residual(accs, extras):
    return extras[0] + accs[0]


def _ep_gated_sum(accs, extras):
    gates = [_sigmoid(e.astype(F32)) for e in extras]
    return gates[0] * accs[0] + gates[1] * accs[1] + gates[2] * accs[2]


def _ep_qk_norm_rotary(accs, extras):
    acc = accs[0]
    gain, cosf, sinf = extras
    outs = []
    for c in range(acc.shape[1] // LANES):
        blk = acc[:, c * LANES:(c + 1) * LANES]
        ms = jnp.mean(blk * blk, axis=-1, keepdims=True)
        y = blk * lax.rsqrt(ms + NORM_EPS) * gain[:, c * LANES:(c + 1) * LANES]
        outs.append(y * cosf + pltpu.roll(y, LANES // 2, 1) * sinf)
    return jnp.concatenate(outs, axis=-1)


def _ffn(layer, x, xn, w_in, w_out):
    f, d = w_out.shape[1], w_out.shape[2]
    tm_in, tf = 1024, 256
    tm_out, tn = 512, 1024
    h = _matmul("ffn_in", layer, [xn], [(w_in, 0), (w_in, f // tf)], (0, 0), _ep_swiglu,
                f, BF16, tm_in, tf)
    return _matmul("ffn_out", layer, [h], [(w_out, 0)], (0,), _ep_residual_half,
                   d, F32, tm_out, tn, extras=[(x, _tile_spec(tm_out, tn))])


def _attn_kernel(q_ref, k_ref, v_ref, lam_ref, gain_ref, linit_ref, o_ref):
    dh = ATTN_HEAD_DIM
    v = v_ref[...]
    lv = lam_ref[...]
    lam_init = linit_ref[...]
    s01 = jnp.sum(lv[0:1] * lv[1:2], axis=-1, keepdims=True)
    s23 = jnp.sum(lv[2:3] * lv[3:4], axis=-1, keepdims=True)
    lam = jnp.exp(s01) - jnp.exp(s23) + lam_init

    def one_map(m):
        s = _dot_nt(q_ref[:, m * dh:(m + 1) * dh], k_ref[:, m * dh:(m + 1) * dh])
        p = jnp.exp2(s - jnp.max(s, axis=-1, keepdims=True))
        inv_l = 1.0 / jnp.sum(p, axis=-1, keepdims=True)
        return _dot(p.astype(BF16), v), inv_l

    o1, r1 = one_map(0)
    o2, r2 = one_map(1)
    o = o1 * r1 - o2 * (lam * r2)
    ms = jnp.mean(o * o, axis=-1, keepdims=True)
    o = o * lax.rsqrt(ms + NORM_EPS) * gain_ref[...] * (1.0 - lam_init)
    o_ref[...] = o.astype(o_ref.dtype)


def _attention(qk, v, lam_vec, sub_gain, lam_init, batch, seq, tq=1024):
    t = qk.shape[0]
    hd = 2 * ATTN_HEAD_DIM
    nq = seq // tq
    return pl.pallas_call(
        _attn_kernel,
        grid=(batch, ATTN_HEADS, nq),
        in_specs=[
            pl.BlockSpec((tq, hd), lambda b, h, i: (b * nq + i, h)),
            pl.BlockSpec((seq, hd), lambda b, h, i: (b, ATTN_HEADS + h)),
            pl.BlockSpec((seq, ATTN_V_DIM), lambda b, h, i: (b, h)),
            pl.BlockSpec((4, ATTN_HEAD_DIM), lambda b, h, i: (0, 0)),
            pl.BlockSpec((1, ATTN_V_DIM), lambda b, h, i: (0, 0)),
            pl.BlockSpec((1, 1), lambda b, h, i: (0, 0)),
        ],
        out_specs=pl.BlockSpec((tq, ATTN_V_DIM), lambda b, h, i: (b * nq + i, h)),
        out_shape=jax.ShapeDtypeStruct((t, ATTN_HEADS * ATTN_V_DIM), BF16),
        compiler_params=_params("parallel", "parallel", "arbitrary"),
        name="diff_attention",
    )(qk, qk, v, lam_vec, sub_gain.reshape(1, -1), lam_init.reshape(1, 1))


def _scan_constants(c, rev):
    idx = np.arange(c)
    if rev:
        cum = idx[None, :] >= idx[:, None]
    else:
        cum = idx[None, :] <= idx[:, None]
    valids = [np.eye(c, dtype=np.float32)]
    s = c // 2
    while s >= 1:
        blk = idx // (2 * s)
        late = (idx // s) % 2
        same = blk[:, None] == blk[None, :]
        if not rev:
            valid = same & (late[:, None] == 1) & (late[None, :] == 0)
        else:
            valid = same & (late[:, None] == 0) & (late[None, :] == 1)
        valids.append(valid.astype(np.float32))
        s //= 2
    return (jnp.asarray(cum.astype(np.float32), dtype=BF16),
            jnp.asarray(np.stack(valids), dtype=F32))


def _level_masks(c, dk, n_lvl, rev):
    row = lax.broadcasted_iota(jnp.int32, (c, dk), 0)
    masks = []
    for lvl in range(1, n_lvl + 1):
        late = ((row >> (n_lvl - lvl)) & 1) == 1
        masks.append(jnp.logical_not(late) if rev else late)
    return masks


def _mid_row(cum, s):
    c, dk = cum.shape
    if s >= 8:
        parts = [jnp.broadcast_to(cum[b * 2 * s + s:b * 2 * s + s + 1, :], (2 * s, dk))
                 for b in range(c // (2 * s))]
        return parts[0] if len(parts) == 1 else jnp.concatenate(parts, axis=0)
    x3 = cum.reshape(c // 8, 8, dk)
    sub = lax.broadcasted_iota(jnp.int32, x3.shape, 1)

    def row(r):
        return jnp.broadcast_to(x3[:, r:r + 1, :], x3.shape)

    if s == 4:
        r3 = row(4)
    elif s == 2:
        r3 = jnp.where(sub < 4, row(2), row(6))
    else:
        r3 = jnp.where(sub < 2, row(1), jnp.where(sub < 4, row(3), jnp.where(sub < 6, row(5), row(7))))
    return r3.reshape(c, dk)


def _scan_chunk(q, k, v, g, st, cum_ref, valid_ref, masks, rev):
    c = q.shape[0]
    n_lvl = len(masks)
    tri = cum_ref[...]
    gh, gm, gl = _split3(g)
    cum = _dot(tri, gh) + _dot(tri, gm) + _dot(tri, gl)
    total = cum[0:1] if rev else cum[c - 1:c]
    a = valid_ref[0] * _dot_nt(q.astype(BF16), k.astype(BF16))
    for lvl in range(1, n_lvl + 1):
        e = -jnp.abs(cum - _mid_row(cum, c >> lvl))
        x = (jnp.where(masks[lvl - 1], q, k) * jnp.exp2(e)).astype(BF16)
        a = a + valid_ref[lvl] * _dot_nt(x, x)
    qd = (q * jnp.exp2(cum)).astype(BF16)
    kd = (k * jnp.exp2(total - cum)).astype(BF16)
    vb = v.astype(BF16)
    o = _dot(a.astype(BF16), vb) + _dot_nt(qd, st.astype(BF16))
    return o, st * jnp.exp2(total) + _dot_tn(vb, kd)


def _gla_gate(lr, w2, bias):
    lr_hi = lr.astype(BF16)
    lr_lo = (lr - lr_hi.astype(F32)).astype(BF16)
    w2_hi = w2.astype(BF16)
    w2_lo = (w2 - w2_hi.astype(F32)).astype(BF16)
    z = _dot(lr_hi, w2_hi) + _dot(lr_hi, w2_lo) + _dot(lr_lo, w2_hi) + bias
    return _log_sigmoid(z) * (LOG2_E / GLA_GATE_NORMALIZER)


def _gla_kernel(qf_ref, kf_ref, vf_ref, lrf_ref, qb_ref, kb_ref, vb_ref, lrb_ref,
                w2f_ref, bf_ref, w2b_ref, bb_ref, dallf_ref, validf_ref, dallb_ref, validb_ref,
                of_ref, ob_ref, stf_ref, stb_ref, *, chunk):
    @pl.when(pl.program_id(2) == 0)
    def _():
        stf_ref[...] = jnp.zeros_like(stf_ref)
        stb_ref[...] = jnp.zeros_like(stb_ref)

    r = GLA_GATE_RANK
    gf_all = _gla_gate(lrf_ref[...][:, 0:r], w2f_ref[...], bf_ref[...])
    gb_all = _gla_gate(lrb_ref[...][:, r:2 * r], w2b_ref[...], bb_ref[...])
    n_lvl = validf_ref.shape[0] - 1
    mf = _level_masks(chunk, GLA_K_DIM, n_lvl, False)
    mb = _level_masks(chunk, GLA_K_DIM, n_lvl, True)
    scale = GLA_K_DIM ** -0.5
    n = qf_ref.shape[0] // chunk
    stf, stb = stf_ref[...], stb_ref[...]
    for ci in range(n):
        sf = slice(ci * chunk, (ci + 1) * chunk)
        sb = slice((n - 1 - ci) * chunk, (n - ci) * chunk)
        o, stf = _scan_chunk(qf_ref[sf, :] * scale, kf_ref[sf, :], vf_ref[sf, :], gf_all[sf, :],
                             stf, dallf_ref, validf_ref, mf, False)
        of_ref[sf, :] = o
        o, stb = _scan_chunk(qb_ref[sb, :] * scale, kb_ref[sb, :], vb_ref[sb, :], gb_all[sb, :],
                             stb, dallb_ref, validb_ref, mb, True)
        ob_ref[sb, :] = o
    stf_ref[...] = stf
    stb_ref[...] = stb


def _hgrn_lower_bound(logits, lmask):
    pe = jnp.exp(logits - jnp.max(logits, axis=0, keepdims=True))
    p = pe / jnp.sum(pe, axis=0, keepdims=True)
    lb = jnp.sum(p * lmask, axis=0, keepdims=True) - p[0:1]
    lb = jnp.clip(lb, 0.0, 1.0 - 1e-6)
    return lb, jnp.log(jnp.maximum(lb, LB_FLOOR)), jnp.log1p(-lb)


def _hgrn_gate(z, lb, log_lb, log_1m):
    t2 = log_1m + _log_sigmoid(z)
    g = jnp.maximum(log_lb, t2) + jnp.log1p(jnp.exp(-jnp.abs(log_lb - t2)))
    return (1.0 - lb) * _sigmoid(-z), g * LOG2_E


def _hgrn_kernel(qf_ref, zf_ref, vf_ref, qb_ref, zb_ref, vb_ref, lbf_ref, lbb_ref, lmask_ref,
                 dallf_ref, validf_ref, dallb_ref, validb_ref, of_ref, ob_ref, stf_ref, stb_ref,
                 *, chunk):
    @pl.when(pl.program_id(2) == 0)
    def _():
        stf_ref[...] = jnp.zeros_like(stf_ref)
        stb_ref[...] = jnp.zeros_like(stb_ref)

    lbf = _hgrn_lower_bound(lbf_ref[...], lmask_ref[...])
    lbb = _hgrn_lower_bound(lbb_ref[...], lmask_ref[...])
    n_lvl = validf_ref.shape[0] - 1
    mf = _level_masks(chunk, HGRN_K_DIM, n_lvl, False)
    mb = _level_masks(chunk, HGRN_K_DIM, n_lvl, True)
    scale = HGRN_K_DIM ** -0.5
    n = qf_ref.shape[0] // chunk
    stf, stb = stf_ref[...], stb_ref[...]
    for ci in range(n):
        sf = slice(ci * chunk, (ci + 1) * chunk)
        sb = slice((n - 1 - ci) * chunk, (n - ci) * chunk)
        k, g = _hgrn_gate(zf_ref[sf, :], *lbf)
        o, stf = _scan_chunk(qf_ref[sf, :] * scale, k, vf_ref[sf, :], g,
                             stf, dallf_ref, validf_ref, mf, False)
        of_ref[sf, :] = o
        k, g = _hgrn_gate(zb_ref[sb, :], *lbb)
        o, stb = _scan_chunk(qb_ref[sb, :] * scale, k, vb_ref[sb, :], g,
                             stb, dallb_ref, validb_ref, mb, True)
        ob_ref[sb, :] = o
    stf_ref[...] = stf
    stb_ref[...] = stb


def _fwd_idx(nblk, col):
    return lambda b, h, c: (b * nblk + c, col(h))


def _bwd_idx(nblk, col):
    return lambda b, h, c: (b * nblk + (nblk - 1 - c), col(h))


def _const_spec(arr):
    zeros = (0,) * arr.ndim
    return pl.BlockSpec(arr.shape, lambda b, h, c: zeros)


def _gla_scan(p_gla, lr, w2f, bf, w2b, bb, batch, seq):
    t = p_gla.shape[0]
    tb, c = SCAN_BLOCK, SCAN_CHUNK
    nblk = seq // tb
    dk, dv, nh = GLA_K_DIM, GLA_V_DIM, GLA_HEADS
    consts = _scan_constants(c, False) + _scan_constants(c, True)
    tok_specs = []
    for idx in (_fwd_idx, _bwd_idx):
        tok_specs += [
            pl.BlockSpec((tb, dk), idx(nblk, lambda h: h)),
            pl.BlockSpec((tb, dk), idx(nblk, lambda h: nh + h)),
            pl.BlockSpec((tb, dv), idx(nblk, lambda h: (2 * nh * dk) // dv + h)),
            pl.BlockSpec((tb, lr.shape[1]), idx(nblk, lambda h: 0)),
        ]
    gate_specs = [pl.BlockSpec((GLA_GATE_RANK, dk), lambda b, h, c_: (0, h)),
                  pl.BlockSpec((1, dk), lambda b, h, c_: (0, h))] * 2
    out_sd = jax.ShapeDtypeStruct((t, nh * dv), F32)
    return pl.pallas_call(
        functools.partial(_gla_kernel, chunk=c),
        grid=(batch, nh, nblk),
        in_specs=tok_specs + gate_specs + [_const_spec(a) for a in consts],
        out_specs=[pl.BlockSpec((tb, dv), _fwd_idx(nblk, lambda h: h)),
                   pl.BlockSpec((tb, dv), _bwd_idx(nblk, lambda h: h))],
        out_shape=[out_sd, out_sd],
        scratch_shapes=[pltpu.VMEM((dv, dk), F32), pltpu.VMEM((dv, dk), F32)],
        compiler_params=_params("parallel", "parallel", "arbitrary"),
        name="gla_scan",
    )(p_gla, p_gla, p_gla, lr, p_gla, p_gla, p_gla, lr,
      w2f, bf.reshape(1, -1), w2b, bb.reshape(1, -1), *consts)


def _hgrn_scan(p_hg, lb_fwd, lb_bwd, lmask, batch, seq):
    t = p_hg.shape[0]
    tb, c = SCAN_BLOCK, SCAN_CHUNK
    nblk = seq // tb
    dk, dv, nh = HGRN_K_DIM, HGRN_V_DIM, HGRN_HEADS
    consts = _scan_constants(c, False) + _scan_constants(c, True)
    n_layers = lb_fwd.shape[0]
    tok_specs = []
    for idx, z_col in ((_fwd_idx, nh), (_bwd_idx, 2 * nh)):
        tok_specs += [
            pl.BlockSpec((tb, dk), idx(nblk, lambda h: h)),
            pl.BlockSpec((tb, dk), idx(nblk, functools.partial(lambda h, z: z + h, z=z_col))),
            pl.BlockSpec((tb, dv), idx(nblk, lambda h: (3 * nh * dk) // dv + h)),
        ]
    lb_specs = [pl.BlockSpec((n_layers, dk), lambda b, h, c_: (0, h)),
                pl.BlockSpec((n_layers, dk), lambda b, h, c_: (0, h)),
                pl.BlockSpec((n_layers, 1), lambda b, h, c_: (0, 0))]
    out_sd = jax.ShapeDtypeStruct((t, nh * dv), F32)
    return pl.pallas_call(
        functools.partial(_hgrn_kernel, chunk=c),
        grid=(batch, nh, nblk),
        in_specs=tok_specs + lb_specs + [_const_spec(a) for a in consts],
        out_specs=[pl.BlockSpec((tb, dv), _fwd_idx(nblk, lambda h: h)),
                   pl.BlockSpec((tb, dv), _bwd_idx(nblk, lambda h: h))],
        out_shape=[out_sd, out_sd],
        scratch_shapes=[pltpu.VMEM((dv, dk), F32), pltpu.VMEM((dv, dk), F32)],
        compiler_params=_params("parallel", "parallel", "arbitrary"),
        name="hgrn_scan",
    )(p_hg, p_hg, p_hg, p_hg, p_hg, p_hg, lb_fwd, lb_bwd, lmask, *consts)


def _headnorm_gate_kernel(of_ref, ob_ref, r_ref, gain_ref, o_ref, *, hd):
    o = of_ref[...] + ob_ref[...]
    gain = gain_ref[...]
    outs = []
    for h in range(o.shape[1] // hd):
        blk = o[:, h * hd:(h + 1) * hd]
        ms = jnp.mean(blk * blk, axis=-1, keepdims=True)
        outs.append(blk * lax.rsqrt(ms + NORM_EPS) * gain)
    y = jnp.concatenate(outs, axis=-1) * _silu(r_ref[...])
    o_ref[...] = y.astype(o_ref.dtype)


def _headnorm_gate(o_f, o_b, p, r_off, gain, hd, tm=512):
    t, w = o_f.shape
    return pl.pallas_call(
        functools.partial(_headnorm_gate_kernel, hd=hd),
        grid=(t // tm,),
        in_specs=[pl.BlockSpec((tm, w), lambda i: (i, 0)),
                  pl.BlockSpec((tm, w), lambda i: (i, 0)),
                  pl.BlockSpec((tm, w), lambda i: (i, r_off)),
                  pl.BlockSpec((1, hd), lambda i: (0, 0))],
        out_specs=pl.BlockSpec((tm, w), lambda i: (i, 0)),
        out_shape=jax.ShapeDtypeStruct((t, w), BF16),
        compiler_params=_params("parallel"),
        name="headnorm_gate",
    )(o_f, o_b, p, gain.reshape(1, hd))


def _rotary_tables(seq):
    half = ATTN_HEAD_DIM // 2
    inv_freq = ROPE_THETA ** (-jnp.arange(half, dtype=F32) / half)
    ang = jnp.arange(seq, dtype=jnp.int32).astype(F32)[:, None] * inv_freq[None, :]
    cos, sin = jnp.cos(ang), jnp.sin(ang)
    return jnp.concatenate([cos, cos], axis=-1), jnp.concatenate([-sin, sin], axis=-1)


def kernel(x, ffn1_norm, ffn1_w_in, ffn1_w_out, mix_norm, w_in, attn_q_norm, attn_k_norm, attn_lambda, attn_sub_norm, gla_gate_w2_fwd, gla_gate_b_fwd, gla_gate_w2_bwd, gla_gate_b_bwd, gla_out_norm, hgrn_lb_fwd, hgrn_lb_bwd, hgrn_out_norm, w_branch_attn, w_branch_gla, w_branch_hgrn, w_out, ffn2_norm, ffn2_w_in, ffn2_w_out):
    batch, seq, d = x.shape
    t = batch * seq
    depth = w_in.shape[0]
    cosf, sinf = _rotary_tables(seq)

    qk_w = 2 * ATTN_HEADS * 2 * ATTN_HEAD_DIM
    av_w = ATTN_HEADS * ATTN_V_DIM
    gla_w = 2 * GLA_HEADS * GLA_K_DIM + 2 * GLA_HEADS * GLA_V_DIM
    lr_w = 2 * GLA_GATE_RANK
    hg_w = 3 * HGRN_HEADS * HGRN_K_DIM + 2 * HGRN_HEADS * HGRN_V_DIM
    c_av, c_gla = qk_w, qk_w + av_w
    c_lr = c_gla + gla_w
    c_tail = c_lr + lr_w

    lmasks = np.tril(np.ones((depth, depth), np.float32))[:, :, None]

    tm, tn = 1024, 512
    tm_w, tn_w = 512, 2048
    q_scale = ATTN_HEAD_DIM ** -0.5 * LOG2_E
    n_sub = qk_w // (2 * ATTN_HEAD_DIM)
    nrow = seq // tm_w
    rot_spec = pl.BlockSpec((tm_w, LANES), lambda j, i: (i % nrow, 0))

    w_in_t = jnp.swapaxes(w_in, 1, 2)
    w_qk_t = w_in_t[:, :qk_w, :].astype(BF16)

    xc = x.reshape(t, d)
    for l in range(depth):
        lam_init = jnp.full((1, 1), 0.8 - 0.6 * math.exp(-0.3 * l), F32)
        xc = _ffn(l, xc, _rmsnorm(xc, ffn1_norm[l]), ffn1_w_in, ffn1_w_out)

        hn = _rmsnorm(xc, mix_norm[l])
        qk_gain = jnp.concatenate([jnp.tile(attn_q_norm[l] * q_scale, n_sub),
                                   jnp.tile(attn_k_norm[l], n_sub)]).reshape(1, qk_w)
        qk = _matmul("proj_qk", l, [hn], [(w_qk_t, 0)], (0,), _ep_qk_norm_rotary,
                     qk_w, BF16, tm_w, tn_w, transposed=True,
                     extras=[(qk_gain, _row_spec(tn_w)), (cosf, rot_spec), (sinf, rot_spec)])
        proj = functools.partial(_matmul, layer=l, a_list=[hn], pairs=(0,), epilogue=_ep_plain,
                                 tm=tm, tn=tn, transposed=True)
        p_av = proj("proj_av", w_list=[(w_in_t, c_av)], n_out=av_w, out_dtype=BF16)
        p_gla = proj("proj_gla", w_list=[(w_in_t, c_gla)], n_out=gla_w, out_dtype=F32)
        p_lr = proj("proj_lr", w_list=[(w_in_t, c_lr)], n_out=LANES, out_dtype=F32, tn=LANES)
        p_hg = proj("proj_hgrn", w_list=[(w_in_t, c_tail)], n_out=hg_w, out_dtype=F32)
        p_gate = proj("proj_gate", w_list=[(w_in_t, c_tail + hg_w)], n_out=3 * d, out_dtype=BF16)

        o_a = _attention(qk, p_av, attn_lambda[l], attn_sub_norm[l], lam_init, batch, seq)

        og_f, og_b = _gla_scan(p_gla, p_lr, gla_gate_w2_fwd[l], gla_gate_b_fwd[l],
                               gla_gate_w2_bwd[l], gla_gate_b_bwd[l], batch, seq)
        u_g = _headnorm_gate(og_f, og_b, p_gla, gla_w // (GLA_HEADS * GLA_V_DIM) - 1, gla_out_norm[l],
                             GLA_V_DIM)

        oh_f, oh_b = _hgrn_scan(p_hg, hgrn_lb_fwd, hgrn_lb_bwd, jnp.asarray(lmasks[l]), batch, seq)
        u_h = _headnorm_gate(oh_f, oh_b, p_hg, hg_w // (HGRN_HEADS * HGRN_V_DIM) - 1, hgrn_out_norm[l],
                             HGRN_V_DIM)

        merged = _matmul(
            "merge", l, [o_a, u_g, u_h],
            [(w_branch_attn, 0), (w_branch_gla, 0), (w_branch_hgrn, 0)], (0, 1, 2),
            _ep_gated_sum, d, BF16, tm, tn,
            extras=[(p_gate, _tile_spec(tm, tn, b * d // tn)) for b in range(3)])
        xc = _matmul("out_proj", l, [merged], [(w_out, 0)], (0,), _ep_residual, d, F32, tm, tn,
                     extras=[(xc, _tile_spec(tm, tn))])

        xc = _ffn(l, xc, _rmsnorm(xc, ffn2_norm[l]), ffn2_w_in, ffn2_w_out)
    return xc.reshape(batch, seq, d)
```

```python
import functools
import math

import numpy as np
import jax
import jax.numpy as jnp
from jax import lax
from jax.experimental import pallas as pl
from jax.experimental.pallas import tpu as pltpu

F32 = jnp.float32
BF16 = jnp.bfloat16

NORM_EPS = 1e-6
LB_FLOOR = 1e-20
ROPE_THETA = 10000.0
GLA_GATE_NORMALIZER = 16.0

ATTN_HEADS = 8
ATTN_HEAD_DIM = 128
ATTN_V_DIM = 256
GLA_HEADS = 4
GLA_K_DIM = 128
GLA_V_DIM = 256
GLA_GATE_RANK = 16
HGRN_HEADS = 8
HGRN_K_DIM = 128
HGRN_V_DIM = 128

LANES = 128
VMEM_LIMIT_BYTES = 58 * 2**20
SCAN_CHUNK = 128
SCAN_BLOCK = 512
LOG2_E = math.log2(math.e)


def _params(*sem):
    return pltpu.CompilerParams(dimension_semantics=sem, vmem_limit_bytes=VMEM_LIMIT_BYTES)


def _dot(a, b):
    return jnp.dot(a, b, preferred_element_type=F32)


def _dot_nt(a, b):
    return lax.dot_general(a, b, (((1,), (1,)), ((), ())), preferred_element_type=F32)


def _dot_tn(a, b):
    return lax.dot_general(a, b, (((0,), (0,)), ((), ())), preferred_element_type=F32)


def _sigmoid(x):
    return 1.0 / (1.0 + jnp.exp(-x))


def _silu(x):
    return x * _sigmoid(x)


def _log_sigmoid(x):
    return jnp.minimum(x, 0.0) - jnp.log1p(jnp.exp(-jnp.abs(x)))


def _split3(x):
    hi = x.astype(BF16)
    r1 = x - hi.astype(F32)
    mid = r1.astype(BF16)
    lo = (r1 - mid.astype(F32)).astype(BF16)
    return hi, mid, lo


def _rmsnorm_kernel(x_ref, g_ref, o_ref, rs_ref):
    x = x_ref[...]
    ms = jnp.mean(x * x, axis=-1, keepdims=True)
    o_ref[...] = (x * g_ref[...]).astype(o_ref.dtype)
    rs_ref[...] = jnp.broadcast_to(lax.rsqrt(ms + NORM_EPS), rs_ref.shape)


def _rmsnorm(x, gain, tm=256):
    t, d = x.shape
    return pl.pallas_call(
        _rmsnorm_kernel,
        grid=(t // tm,),
        in_specs=[pl.BlockSpec((tm, d), lambda i: (i, 0)),
                  pl.BlockSpec((1, d), lambda i: (0, 0))],
        out_specs=[pl.BlockSpec((tm, d), lambda i: (i, 0)),
                   pl.BlockSpec((tm, LANES), lambda i: (i, 0))],
        out_shape=[jax.ShapeDtypeStruct((t, d), BF16), jax.ShapeDtypeStruct((t, LANES), F32)],
        compiler_params=_params("parallel"),
        name="rmsnorm",
    )(x, gain.reshape(1, d))


def _mm_kernel(*refs, n_a, pairs, needs_cast, n_extra, transposed, epilogue):
    n_w = len(pairs)
    a_refs = refs[:n_a]
    w_refs = refs[n_a:n_a + n_w]
    e_refs = refs[n_a + n_w:n_a + n_w + n_extra]
    o_ref = refs[n_a + n_w + n_extra]
    scratch = refs[n_a + n_w + n_extra + 1:]
    w_bf = []
    si = 0
    for k in range(n_w):
        if needs_cast[k]:
            sc = scratch[si]
            si += 1

            @pl.when(pl.program_id(1) == 0)
            def _(sc=sc, w_ref=w_refs[k]):
                sc[...] = w_ref[...].reshape(sc.shape).astype(BF16)

            w_bf.append(sc)
        else:
            w_bf.append(w_refs[k])
    dot = _dot_nt if transposed else _dot
    accs = [dot(a_refs[pairs[k]][...], w_bf[k][...].reshape(w_bf[k].shape[-2:])) for k in range(n_w)]
    o_ref[...] = epilogue(accs, [e[...] for e in e_refs]).astype(o_ref.dtype)


def _matmul(name, layer, a_list, w_list, pairs, epilogue, n_out, out_dtype, tm, tn, extras=(),
            transposed=False):
    t = a_list[0].shape[0]
    in_specs, args, needs_cast, scratch = [], [], [], []
    for a in a_list:
        in_specs.append(pl.BlockSpec((tm, a.shape[1]), lambda j, i: (i, 0)))
        args.append(a)
    for w, off in w_list:
        at = (layer,) if w.ndim == 3 else ()
        if transposed:
            blk = (tn, w.shape[-1])
            spec = pl.BlockSpec(tuple(pl.Element(n) for n in (1,) * len(at) + blk),
                                functools.partial(lambda j, i, o, at: at + (pl.multiple_of(o + j * tn, 32), 0),
                                                  o=off, at=at))
        else:
            blk = (w.shape[-2], tn)
            spec = pl.BlockSpec((None,) * len(at) + blk,
                                functools.partial(lambda j, i, o, at: at + (0, j + o), o=off, at=at))
        in_specs.append(spec)
        if w.dtype != BF16:
            scratch.append(pltpu.VMEM(blk, BF16))
        needs_cast.append(w.dtype != BF16)
        args.append(w)
    for e, spec in extras:
        in_specs.append(spec)
        args.append(e)
    return pl.pallas_call(
        functools.partial(_mm_kernel, n_a=len(a_list), pairs=tuple(pairs), needs_cast=tuple(needs_cast),
                          n_extra=len(extras), transposed=transposed, epilogue=epilogue),
        grid=(n_out // tn, t // tm),
        in_specs=in_specs,
        out_specs=pl.BlockSpec((tm, tn), lambda j, i: (i, j)),
        out_shape=jax.ShapeDtypeStruct((t, n_out), out_dtype),
        scratch_shapes=scratch,
        compiler_params=_params("parallel", "arbitrary"),
        name=name,
    )(*args)


def _tile_spec(tm, tn, off=0):
    return pl.BlockSpec((tm, tn), functools.partial(lambda j, i, o: (i, j + o), o=off))


def _row_spec(tn):
    return pl.BlockSpec((1, tn), lambda j, i: (0, j))


def _rs_spec(tm):
    return pl.BlockSpec((tm, LANES), lambda j, i: (i, 0))


def _lane_tile(rs, n):
    return rs if n == LANES else jnp.concatenate([rs] * (n // LANES), axis=1)


def _ep_scaled(accs, extras):
    return accs[0] * _lane_tile(extras[0], accs[0].shape[1])


def _ep_swiglu(accs, extras):
    rs = _lane_tile(extras[0], accs[0].shape[1])
    return _silu(accs[0] * rs) * (accs[1] * rs)


def _ep_gated_sum(accs, extras):
    gates = [_sigmoid(e.astype(F32)) for e in extras]
    return gates[0] * accs[0] + gates[1] * accs[1] + gates[2] * accs[2]


def _ep_qk_norm_rotary(accs, extras):
    acc = accs[0]
    gain, cosf, sinf, rs = extras
    outs = []
    for c in range(acc.shape[1] // LANES):
        blk = acc[:, c * LANES:(c + 1) * LANES] * rs
        ms = jnp.mean(blk * blk, axis=-1, keepdims=True)
        y = blk * lax.rsqrt(ms + NORM_EPS) * gain[:, c * LANES:(c + 1) * LANES]
        outs.append(y * cosf + pltpu.roll(y, LANES // 2, 1) * sinf)
    return jnp.concatenate(outs, axis=-1)


def _residual_kernel(a_ref, w_ref, x_ref, g_ref, o_ref, xg_ref, rs_ref, wbf_ref, ss_ref, *, scale, d_model):
    j, i = pl.program_id(0), pl.program_id(1)
    tm, tn = o_ref.shape

    @pl.when(i == 0)
    def _():
        wbf_ref[...] = w_ref[...].astype(BF16)

    xn = x_ref[...] + scale * _dot(a_ref[...], wbf_ref[...])
    o_ref[...] = xn
    xg_ref[...] = (xn * g_ref[...]).astype(xg_ref.dtype)
    sq = xn * xn
    part = sq[:, 0:LANES]
    for c in range(1, tn // LANES):
        part = part + sq[:, c * LANES:(c + 1) * LANES]
    part = jnp.broadcast_to(jnp.sum(part, axis=-1, keepdims=True), (tm, LANES))
    rows = pl.ds(pl.multiple_of(i * tm, tm), tm)

    @pl.when(j == 0)
    def _():
        ss_ref[rows, :] = part

    @pl.when(j != 0)
    def _():
        ss_ref[rows, :] = ss_ref[rows, :] + part

    rs_ref[...] = lax.rsqrt(ss_ref[rows, :] * (1.0 / d_model) + NORM_EPS)


def _residual_matmul(name, layer, a, w, x, gain_next, scale, tm, tn):
    t, d = x.shape
    k_dim = w.shape[1]
    x_new, xg, rs_partial = pl.pallas_call(
        functools.partial(_residual_kernel, scale=scale, d_model=d),
        grid=(d // tn, t // tm),
        in_specs=[pl.BlockSpec((tm, k_dim), lambda j, i: (i, 0)),
                  pl.BlockSpec((None, k_dim, tn), lambda j, i: (layer, 0, j)),
                  pl.BlockSpec((tm, tn), lambda j, i: (i, j)),
                  pl.BlockSpec((1, tn), lambda j, i: (0, j))],
        out_specs=[pl.BlockSpec((tm, tn), lambda j, i: (i, j)),
                   pl.BlockSpec((tm, tn), lambda j, i: (i, j)),
                   pl.BlockSpec((None, tm, LANES), lambda j, i: (j, i, 0))],
        out_shape=[jax.ShapeDtypeStruct((t, d), F32), jax.ShapeDtypeStruct((t, d), BF16),
                   jax.ShapeDtypeStruct((d // tn, t, LANES), F32)],
        scratch_shapes=[pltpu.VMEM((k_dim, tn), BF16), pltpu.VMEM((t, LANES), F32)],
        compiler_params=_params("arbitrary", "arbitrary"),
        name=name,
    )(a, w, x, gain_next.reshape(1, d))
    return x_new, xg, rs_partial[d // tn - 1]


def _ffn(layer, x, xg, rs, w_in, w_out, gain_next):
    f = w_out.shape[1]
    tm, tf = 512, 512
    h = _matmul("ffn_in", layer, [xg], [(w_in, 0), (w_in, f // tf)], (0, 0), _ep_swiglu,
                f, BF16, tm, tf, extras=[(rs, _rs_spec(tm))])
    return _residual_matmul("ffn_out", layer, h, w_out, x, gain_next, 0.5, 512, 1024)


def _attn_kernel(q_ref, k_ref, v_ref, lam_ref, gain_ref, linit_ref, o_ref):
    dh = ATTN_HEAD_DIM
    v = v_ref[...]
    lv = lam_ref[...]
    lam_init = linit_ref[...]
    s01 = jnp.sum(lv[0:1] * lv[1:2], axis=-1, keepdims=True)
    s23 = jnp.sum(lv[2:3] * lv[3:4], axis=-1, keepdims=True)
    lam = jnp.exp(s01) - jnp.exp(s23) + lam_init

    def one_map(m):
        s = _dot_nt(q_ref[:, m * dh:(m + 1) * dh], k_ref[:, m * dh:(m + 1) * dh])
        p = jnp.exp2(s - jnp.max(s, axis=-1, keepdims=True))
        inv_l = 1.0 / jnp.sum(p, axis=-1, keepdims=True)
        return _dot(p.astype(BF16), v), inv_l

    o1, r1 = one_map(0)
    o2, r2 = one_map(1)
    o = o1 * r1 - o2 * (lam * r2)
    ms = jnp.mean(o * o, axis=-1, keepdims=True)
    o = o * lax.rsqrt(ms + NORM_EPS) * gain_ref[...] * (1.0 - lam_init)
    o_ref[...] = o.astype(o_ref.dtype)


def _attention(qk, v, lam_vec, sub_gain, lam_init, batch, seq, tq=1024):
    t = qk.shape[0]
    hd = 2 * ATTN_HEAD_DIM
    nq = seq // tq
    return pl.pallas_call(
        _attn_kernel,
        grid=(batch, ATTN_HEADS, nq),
        in_specs=[
            pl.BlockSpec((tq, hd), lambda b, h, i: (b * nq + i, h)),
            pl.BlockSpec((seq, hd), lambda b, h, i: (b, ATTN_HEADS + h)),
            pl.BlockSpec((seq, ATTN_V_DIM), lambda b, h, i: (b, h)),
            pl.BlockSpec((4, ATTN_HEAD_DIM), lambda b, h, i: (0, 0)),
            pl.BlockSpec((1, ATTN_V_DIM), lambda b, h, i: (0, 0)),
            pl.BlockSpec((1, 1), lambda b, h, i: (0, 0)),
        ],
        out_specs=pl.BlockSpec((tq, ATTN_V_DIM), lambda b, h, i: (b * nq + i, h)),
        out_shape=jax.ShapeDtypeStruct((t, ATTN_HEADS * ATTN_V_DIM), BF16),
        compiler_params=_params("parallel", "parallel", "arbitrary"),
        name="diff_attention",
    )(qk, qk, v, lam_vec, sub_gain.reshape(1, -1), lam_init.reshape(1, 1))


def _scan_constants(c, rev):
    idx = np.arange(c)
    if rev:
        cum = idx[None, :] >= idx[:, None]
    else:
        cum = idx[None, :] <= idx[:, None]
    valids = [np.eye(c, dtype=np.float32)]
    s = c // 2
    while s >= 1:
        blk = idx // (2 * s)
        late = (idx // s) % 2
        same = blk[:, None] == blk[None, :]
        if not rev:
            valid = same & (late[:, None] == 1) & (late[None, :] == 0)
        else:
            valid = same & (late[:, None] == 0) & (late[None, :] == 1)
        valids.append(valid.astype(np.float32))
        s //= 2
    return (jnp.asarray(cum.astype(np.float32), dtype=BF16),
            jnp.asarray(np.stack(valids), dtype=F32))


def _level_masks(c, dk, n_lvl, rev):
    row = lax.broadcasted_iota(jnp.int32, (c, dk), 0)
    masks = []
    for lvl in range(1, n_lvl + 1):
        late = ((row >> (n_lvl - lvl)) & 1) == 1
        masks.append(jnp.logical_not(late) if rev else late)
    return masks


def _mid_row(cum, s):
    c, dk = cum.shape
    if s >= 8:
        parts = [jnp.broadcast_to(cum[b * 2 * s + s:b * 2 * s + s + 1, :], (2 * s, dk))
                 for b in range(c // (2 * s))]
        return parts[0] if len(parts) == 1 else jnp.concatenate(parts, axis=0)
    x3 = cum.reshape(c // 8, 8, dk)
    sub = lax.broadcasted_iota(jnp.int32, x3.shape, 1)

    def row(r):
        return jnp.broadcast_to(x3[:, r:r + 1, :], x3.shape)

    if s == 4:
        r3 = row(4)
    elif s == 2:
        r3 = jnp.where(sub < 4, row(2), row(6))
    else:
        r3 = jnp.where(sub < 2, row(1), jnp.where(sub < 4, row(3), jnp.where(sub < 6, row(5), row(7))))
    return r3.reshape(c, dk)


def _scan_chunk(q, k, v, g, st, cum_ref, valid_ref, masks, rev):
    c = q.shape[0]
    n_lvl = len(masks)
    tri = cum_ref[...]
    gh, gm, gl = _split3(g)
    cum = _dot(tri, gh) + _dot(tri, gm) + _dot(tri, gl)
    total = cum[0:1] if rev else cum[c - 1:c]
    a = valid_ref[0] * _dot_nt(q.astype(BF16), k.astype(BF16))
    for lvl in range(1, n_lvl + 1):
        e = -jnp.abs(cum - _mid_row(cum, c >> lvl))
        x = (jnp.where(masks[lvl - 1], q, k) * jnp.exp2(e)).astype(BF16)
        a = a + valid_ref[lvl] * _dot_nt(x, x)
    qd = (q * jnp.exp2(cum)).astype(BF16)
    kd = (k * jnp.exp2(total - cum)).astype(BF16)
    vb = v.astype(BF16)
    o = _dot(a.astype(BF16), vb) + _dot_nt(qd, st.astype(BF16))
    return o, st * jnp.exp2(total) + _dot_tn(vb, kd)


def _gla_gate(lr, w2, bias):
    lr_hi = lr.astype(BF16)
    lr_lo = (lr - lr_hi.astype(F32)).astype(BF16)
    w2_hi = w2.astype(BF16)
    w2_lo = (w2 - w2_hi.astype(F32)).astype(BF16)
    z = _dot(lr_hi, w2_hi) + _dot(lr_hi, w2_lo) + _dot(lr_lo, w2_hi) + bias
    return _log_sigmoid(z) * (LOG2_E / GLA_GATE_NORMALIZER)


def _gla_kernel(qf_ref, kf_ref, vf_ref, lrf_ref, qb_ref, kb_ref, vb_ref, lrb_ref,
                w2f_ref, bf_ref, w2b_ref, bb_ref, dallf_ref, validf_ref, dallb_ref, validb_ref,
                of_ref, ob_ref, stf_ref, stb_ref, *, chunk):
    @pl.when(pl.program_id(2) == 0)
    def _():
        stf_ref[...] = jnp.zeros_like(stf_ref)
        stb_ref[...] = jnp.zeros_like(stb_ref)

    r = GLA_GATE_RANK
    gf_all = _gla_gate(lrf_ref[...][:, 0:r], w2f_ref[...], bf_ref[...])
    gb_all = _gla_gate(lrb_ref[...][:, r:2 * r], w2b_ref[...], bb_ref[...])
    n_lvl = validf_ref.shape[0] - 1
    mf = _level_masks(chunk, GLA_K_DIM, n_lvl, False)
    mb = _level_masks(chunk, GLA_K_DIM, n_lvl, True)
    scale = GLA_K_DIM ** -0.5
    n = qf_ref.shape[0] // chunk
    stf, stb = stf_ref[...], stb_ref[...]
    for ci in range(n):
        sf = slice(ci * chunk, (ci + 1) * chunk)
        sb = slice((n - 1 - ci) * chunk, (n - ci) * chunk)
        o, stf = _scan_chunk(qf_ref[sf, :] * scale, kf_ref[sf, :], vf_ref[sf, :], gf_all[sf, :],
                             stf, dallf_ref, validf_ref, mf, False)
        of_ref[sf, :] = o
        o, stb = _scan_chunk(qb_ref[sb, :] * scale, kb_ref[sb, :], vb_ref[sb, :], gb_all[sb, :],
                             stb, dallb_ref, validb_ref, mb, True)
        ob_ref[sb, :] = o
    stf_ref[...] = stf
    stb_ref[...] = stb


def _hgrn_lower_bound(logits, lmask):
    pe = jnp.exp(logits - jnp.max(logits, axis=0, keepdims=True))
    p = pe / jnp.sum(pe, axis=0, keepdims=True)
    lb = jnp.sum(p * lmask, axis=0, keepdims=True) - p[0:1]
    lb = jnp.clip(lb, 0.0, 1.0 - 1e-6)
    return lb, jnp.log(jnp.maximum(lb, LB_FLOOR)), jnp.log1p(-lb)


def _hgrn_gate(z, lb, log_lb, log_1m):
    t2 = log_1m + _log_sigmoid(z)
    g = jnp.maximum(log_lb, t2) + jnp.log1p(jnp.exp(-jnp.abs(log_lb - t2)))
    return (1.0 - lb) * _sigmoid(-z), g * LOG2_E


def _hgrn_kernel(qf_ref, zf_ref, vf_ref, qb_ref, zb_ref, vb_ref, lbf_ref, lbb_ref, lmask_ref,
                 dallf_ref, validf_ref, dallb_ref, validb_ref, of_ref, ob_ref, stf_ref, stb_ref,
                 *, chunk):
    @pl.when(pl.program_id(2) == 0)
    def _():
        stf_ref[...] = jnp.zeros_like(stf_ref)
        stb_ref[...] = jnp.zeros_like(stb_ref)

    lbf = _hgrn_lower_bound(lbf_ref[...], lmask_ref[...])
    lbb = _hgrn_lower_bound(lbb_ref[...], lmask_ref[...])
    n_lvl = validf_ref.shape[0] - 1
    mf = _level_masks(chunk, HGRN_K_DIM, n_lvl, False)
    mb = _level_masks(chunk, HGRN_K_DIM, n_lvl, True)
    scale = HGRN_K_DIM ** -0.5
    n = qf_ref.shape[0] // chunk
    stf, stb = stf_ref[...], stb_ref[...]
    for ci in range(n):
        sf = slice(ci * chunk, (ci + 1) * chunk)
        sb = slice((n - 1 - ci) * chunk, (n - ci) * chunk)
        k, g = _hgrn_gate(zf_ref[sf, :], *lbf)
        o, stf = _scan_chunk(qf_ref[sf, :] * scale, k, vf_ref[sf, :], g,
                             stf, dallf_ref, validf_ref, mf, False)
        of_ref[sf, :] = o
        k, g = _hgrn_gate(zb_ref[sb, :], *lbb)
        o, stb = _scan_chunk(qb_ref[sb, :] * scale, k, vb_ref[sb, :], g,
                             stb, dallb_ref, validb_ref, mb, True)
        ob_ref[sb, :] = o
    stf_ref[...] = stf
    stb_ref[...] = stb


def _fwd_idx(nblk, col):
    return lambda b, h, c: (b * nblk + c, col(h))


def _bwd_idx(nblk, col):
    return lambda b, h, c: (b * nblk + (nblk - 1 - c), col(h))


def _const_spec(arr):
    zeros = (0,) * arr.ndim
    return pl.BlockSpec(arr.shape, lambda b, h, c: zeros)


def _gla_scan(p_gla, lr, w2f, bf, w2b, bb, batch, seq):
    t = p_gla.shape[0]
    tb, c = SCAN_BLOCK, SCAN_CHUNK
    nblk = seq // tb
    dk, dv, nh = GLA_K_DIM, GLA_V_DIM, GLA_HEADS
    consts = _scan_constants(c, False) + _scan_constants(c, True)
    tok_specs = []
    for idx in (_fwd_idx, _bwd_idx):
        tok_specs += [
            pl.BlockSpec((tb, dk), idx(nblk, lambda h: h)),
            pl.BlockSpec((tb, dk), idx(nblk, lambda h: nh + h)),
            pl.BlockSpec((tb, dv), idx(nblk, lambda h: (2 * nh * dk) // dv + h)),
            pl.BlockSpec((tb, lr.shape[1]), idx(nblk, lambda h: 0)),
        ]
    gate_specs = [pl.BlockSpec((GLA_GATE_RANK, dk), lambda b, h, c_: (0, h)),
                  pl.BlockSpec((1, dk), lambda b, h, c_: (0, h))] * 2
    out_sd = jax.ShapeDtypeStruct((t, nh * dv), F32)
    return pl.pallas_call(
        functools.partial(_gla_kernel, chunk=c),
        grid=(batch, nh, nblk),
        in_specs=tok_specs + gate_specs + [_const_spec(a) for a in consts],
        out_specs=[pl.BlockSpec((tb, dv), _fwd_idx(nblk, lambda h: h)),
                   pl.BlockSpec((tb, dv), _bwd_idx(nblk, lambda h: h))],
        out_shape=[out_sd, out_sd],
        scratch_shapes=[pltpu.VMEM((dv, dk), F32), pltpu.VMEM((dv, dk), F32)],
        compiler_params=_params("parallel", "parallel", "arbitrary"),
        name="gla_scan",
    )(p_gla, p_gla, p_gla, lr, p_gla, p_gla, p_gla, lr,
      w2f, bf.reshape(1, -1), w2b, bb.reshape(1, -1), *consts)


def _hgrn_scan(p_hg, lb_fwd, lb_bwd, lmask, batch, seq):
    t = p_hg.shape[0]
    tb, c = SCAN_BLOCK, SCAN_CHUNK
    nblk = seq // tb
    dk, dv, nh = HGRN_K_DIM, HGRN_V_DIM, HGRN_HEADS
    consts = _scan_constants(c, False) + _scan_constants(c, True)
    n_layers = lb_fwd.shape[0]
    tok_specs = []
    for idx, z_col in ((_fwd_idx, nh), (_bwd_idx, 2 * nh)):
        tok_specs += [
            pl.BlockSpec((tb, dk), idx(nblk, lambda h: h)),
            pl.BlockSpec((tb, dk), idx(nblk, functools.partial(lambda h, z: z + h, z=z_col))),
            pl.BlockSpec((tb, dv), idx(nblk, lambda h: (3 * nh * dk) // dv + h)),
        ]
    lb_specs = [pl.BlockSpec((n_layers, dk), lambda b, h, c_: (0, h)),
                pl.BlockSpec((n_layers, dk), lambda b, h, c_: (0, h)),
                pl.BlockSpec((n_layers, 1), lambda b, h, c_: (0, 0))]
    out_sd = jax.ShapeDtypeStruct((t, nh * dv), F32)
    return pl.pallas_call(
        functools.partial(_hgrn_kernel, chunk=c),
        grid=(batch, nh, nblk),
        in_specs=tok_specs + lb_specs + [_const_spec(a) for a in consts],
        out_specs=[pl.BlockSpec((tb, dv), _fwd_idx(nblk, lambda h: h)),
                   pl.BlockSpec((tb, dv), _bwd_idx(nblk, lambda h: h))],
        out_shape=[out_sd, out_sd],
        scratch_shapes=[pltpu.VMEM((dv, dk), F32), pltpu.VMEM((dv, dk), F32)],
        compiler_params=_params("parallel", "parallel", "arbitrary"),
        name="hgrn_scan",
    )(p_hg, p_hg, p_hg, p_hg, p_hg, p_hg, lb_fwd, lb_bwd, lmask, *consts)


def _headnorm_gate_kernel(of_ref, ob_ref, r_ref, gain_ref, o_ref, *, hd):
    o = of_ref[...] + ob_ref[...]
    gain = gain_ref[...]
    outs = []
    for h in range(o.shape[1] // hd):
        blk = o[:, h * hd:(h + 1) * hd]
        ms = jnp.mean(blk * blk, axis=-1, keepdims=True)
        outs.append(blk * lax.rsqrt(ms + NORM_EPS) * gain)
    y = jnp.concatenate(outs, axis=-1) * _silu(r_ref[...])
    o_ref[...] = y.astype(o_ref.dtype)


def _headnorm_gate(o_f, o_b, p, r_off, gain, hd, tm=512):
    t, w = o_f.shape
    return pl.pallas_call(
        functools.partial(_headnorm_gate_kernel, hd=hd),
        grid=(t // tm,),
        in_specs=[pl.BlockSpec((tm, w), lambda i: (i, 0)),
                  pl.BlockSpec((tm, w), lambda i: (i, 0)),
                  pl.BlockSpec((tm, w), lambda i: (i, r_off)),
                  pl.BlockSpec((1, hd), lambda i: (0, 0))],
        out_specs=pl.BlockSpec((tm, w), lambda i: (i, 0)),
        out_shape=jax.ShapeDtypeStruct((t, w), BF16),
        compiler_params=_params("parallel"),
        name="headnorm_gate",
    )(o_f, o_b, p, gain.reshape(1, hd))


def _rotary_tables(seq):
    half = ATTN_HEAD_DIM // 2
    inv_freq = ROPE_THETA ** (-jnp.arange(half, dtype=F32) / half)
    ang = jnp.arange(seq, dtype=jnp.int32).astype(F32)[:, None] * inv_freq[None, :]
    cos, sin = jnp.cos(ang), jnp.sin(ang)
    return jnp.concatenate([cos, cos], axis=-1), jnp.concatenate([-sin, sin], axis=-1)


def kernel(x, ffn1_norm, ffn1_w_in, ffn1_w_out, mix_norm, w_in, attn_q_norm, attn_k_norm, attn_lambda, attn_sub_norm, gla_gate_w2_fwd, gla_gate_b_fwd, gla_gate_w2_bwd, gla_gate_b_bwd, gla_out_norm, hgrn_lb_fwd, hgrn_lb_bwd, hgrn_out_norm, w_branch_attn, w_branch_gla, w_branch_hgrn, w_out, ffn2_norm, ffn2_w_in, ffn2_w_out):
    batch, seq, d = x.shape
    t = batch * seq
    depth = w_in.shape[0]
    cosf, sinf = _rotary_tables(seq)

    qk_w = 2 * ATTN_HEADS * 2 * ATTN_HEAD_DIM
    av_w = ATTN_HEADS * ATTN_V_DIM
    gla_w = 2 * GLA_HEADS * GLA_K_DIM + 2 * GLA_HEADS * GLA_V_DIM
    lr_w = 2 * GLA_GATE_RANK
    hg_w = 3 * HGRN_HEADS * HGRN_K_DIM + 2 * HGRN_HEADS * HGRN_V_DIM
    c_av, c_gla = qk_w, qk_w + av_w
    c_lr = c_gla + gla_w
    c_tail = c_lr + lr_w

    lmasks = np.tril(np.ones((depth, depth), np.float32))[:, :, None]

    tm, tn = 1024, 512
    tm_p, tn_p = 512, 1024
    tm_w, tn_w = 512, 2048
    q_scale = ATTN_HEAD_DIM ** -0.5 * LOG2_E
    n_sub = qk_w // (2 * ATTN_HEAD_DIM)
    nrow = seq // tm_w
    rot_spec = pl.BlockSpec((tm_w, LANES), lambda j, i: (i % nrow, 0))

    w_in_t = jnp.swapaxes(w_in, 1, 2)
    w_qk_t = w_in_t[:, :qk_w, :].astype(BF16)

    xc = x.reshape(t, d)
    xg, rs = _rmsnorm(xc, ffn1_norm[0])
    for l in range(depth):
        lam_init = jnp.full((1, 1), 0.8 - 0.6 * math.exp(-0.3 * l), F32)
        xc, hn, hrs = _ffn(l, xc, xg, rs, ffn1_w_in, ffn1_w_out, mix_norm[l])

        qk_gain = jnp.concatenate([jnp.tile(attn_q_norm[l] * q_scale, n_sub),
                                   jnp.tile(attn_k_norm[l], n_sub)]).reshape(1, qk_w)
        qk = _matmul("proj_qk", l, [hn], [(w_qk_t, 0)], (0,), _ep_qk_norm_rotary,
                     qk_w, BF16, tm_w, tn_w, transposed=True,
                     extras=[(qk_gain, _row_spec(tn_w)), (cosf, rot_spec), (sinf, rot_spec),
                             (hrs, _rs_spec(tm_w))])
        proj = functools.partial(_matmul, layer=l, a_list=[hn], pairs=(0,), epilogue=_ep_scaled,
                                 tm=tm_p, tn=tn_p, transposed=True, extras=[(hrs, _rs_spec(tm_p))])
        p_av = proj("proj_av", w_list=[(w_in_t, c_av)], n_out=av_w, out_dtype=BF16)
        p_gla = proj("proj_gla", w_list=[(w_in_t, c_gla)], n_out=gla_w, out_dtype=F32)
        p_lr = proj("proj_lr", w_list=[(w_in_t, c_lr)], n_out=LANES, out_dtype=F32, tn=LANES)
        p_hg = proj("proj_hgrn", w_list=[(w_in_t, c_tail)], n_out=hg_w, out_dtype=F32)
        p_gate = proj("proj_gate", w_list=[(w_in_t, c_tail + hg_w)], n_out=3 * d, out_dtype=BF16)

        o_a = _attention(qk, p_av, attn_lambda[l], attn_sub_norm[l], lam_init, batch, seq)

        og_f, og_b = _gla_scan(p_gla, p_lr, gla_gate_w2_fwd[l], gla_gate_b_fwd[l],
                               gla_gate_w2_bwd[l], gla_gate_b_bwd[l], batch, seq)
        u_g = _headnorm_gate(og_f, og_b, p_gla, gla_w // (GLA_HEADS * GLA_V_DIM) - 1, gla_out_norm[l],
                             GLA_V_DIM)

        oh_f, oh_b = _hgrn_scan(p_hg, hgrn_lb_fwd, hgrn_lb_bwd, jnp.asarray(lmasks[l]), batch, seq)
        u_h = _headnorm_gate(oh_f, oh_b, p_hg, hg_w // (HGRN_HEADS * HGRN_V_DIM) - 1, hgrn_out_norm[l],
                             HGRN_V_DIM)

        merged = _matmul(
            "merge", l, [o_a, u_g, u_h],
            [(w_branch_attn, 0), (w_branch_gla, 0), (w_branch_hgrn, 0)], (0, 1, 2),
            _ep_gated_sum, d, BF16, tm, tn,
            extras=[(p_gate, _tile_spec(tm, tn, b * d // tn)) for b in range(3)])
        xc, xg, rs = _residual_matmul("out_proj", l, merged, w_out, xc, ffn2_norm[l], 1.0, tm, tn)

        xc, xg, rs = _ffn(l, xc, xg, rs, ffn2_w_in, ffn2_w_out, ffn1_norm[(l + 1) % depth])
    return xc.reshape(batch, seq, d)
```

```python
import functools
import math

import numpy as np
import jax
import jax.numpy as jnp
from jax import lax
from jax.experimental import pallas as pl
from jax.experimental.pallas import tpu as pltpu

F32 = jnp.float32
BF16 = jnp.bfloat16

NORM_EPS = 1e-6
LB_FLOOR = 1e-20
ROPE_THETA = 10000.0
GLA_GATE_NORMALIZER = 16.0

ATTN_HEADS = 8
ATTN_HEAD_DIM = 128
ATTN_V_DIM = 256
GLA_HEADS = 4
GLA_K_DIM = 128
GLA_V_DIM = 256
GLA_GATE_RANK = 16
HGRN_HEADS = 8
HGRN_K_DIM = 128
HGRN_V_DIM = 128

LANES = 128
VMEM_LIMIT_BYTES = 58 * 2**20
SCAN_CHUNK = 128
SCAN_BLOCK = 512
LOG2_E = math.log2(math.e)


def _params(*sem):
    return pltpu.CompilerParams(dimension_semantics=sem, vmem_limit_bytes=VMEM_LIMIT_BYTES)


def _dot(a, b):
    return jnp.dot(a, b, preferred_element_type=F32)


def _dot_nt(a, b):
    return lax.dot_general(a, b, (((1,), (1,)), ((), ())), preferred_element_type=F32)


def _dot_tn(a, b):
    return lax.dot_general(a, b, (((0,), (0,)), ((), ())), preferred_element_type=F32)


def _sigmoid(x):
    return 1.0 / (1.0 + jnp.exp(-x))


def _silu(x):
    return x * _sigmoid(x)


def _log_sigmoid(x):
    return jnp.minimum(x, 0.0) - jnp.log1p(jnp.exp(-jnp.abs(x)))


def _split3(x):
    hi = x.astype(BF16)
    r1 = x - hi.astype(F32)
    mid = r1.astype(BF16)
    lo = (r1 - mid.astype(F32)).astype(BF16)
    return hi, mid, lo


def _rmsnorm_kernel(x_ref, g_ref, o_ref, rs_ref):
    x = x_ref[...]
    ms = jnp.mean(x * x, axis=-1, keepdims=True)
    o_ref[...] = (x * g_ref[...]).astype(o_ref.dtype)
    rs_ref[...] = jnp.broadcast_to(lax.rsqrt(ms + NORM_EPS), rs_ref.shape)


def _rmsnorm(x, gain, tm=256):
    t, d = x.shape
    return pl.pallas_call(
        _rmsnorm_kernel,
        grid=(t // tm,),
        in_specs=[pl.BlockSpec((tm, d), lambda i: (i, 0)),
                  pl.BlockSpec((1, d), lambda i: (0, 0))],
        out_specs=[pl.BlockSpec((tm, d), lambda i: (i, 0)),
                   pl.BlockSpec((tm, LANES), lambda i: (i, 0))],
        out_shape=[jax.ShapeDtypeStruct((t, d), BF16), jax.ShapeDtypeStruct((t, LANES), F32)],
        compiler_params=_params("parallel"),
        name="rmsnorm",
    )(x, gain.reshape(1, d))


def _weight_block_copy(w_hbm, stage, sem, j, *, layer, off, tn, transposed):
    if transposed:
        src = w_hbm.at[layer, pl.ds(pl.multiple_of(off + j * tn, 32), tn), :]
    else:
        src = w_hbm.at[layer, :, pl.ds(pl.multiple_of((j + off) * tn, LANES), tn)]
    return pltpu.make_async_copy(src, stage, sem)


def _mm_kernel(*refs, n_a, pairs, needs_cast, staged, n_extra, transposed, epilogue):
    n_w = len(pairs)
    a_refs = refs[:n_a]
    w_refs = refs[n_a:n_a + n_w]
    e_refs = refs[n_a + n_w:n_a + n_w + n_extra]
    o_ref = refs[n_a + n_w + n_extra]
    scratch = refs[n_a + n_w + n_extra + 1:]
    w_bf = []
    si = 0
    for k in range(n_w):
        if staged[k] is not None:
            sc, stage, sem = scratch[si:si + 3]
            si += 3
            copy = functools.partial(_weight_block_copy, w_refs[k], stage, sem, **staged[k])

            @pl.when(pl.program_id(1) == 0)
            def _(sc=sc, stage=stage, copy=copy):
                j = pl.program_id(0)

                @pl.when(j == 0)
                def _():
                    copy(j).start()

                copy(j).wait()
                sc[...] = stage[...].astype(BF16)

                @pl.when(j + 1 < pl.num_programs(0))
                def _():
                    copy(j + 1).start()

            w_bf.append(sc)
        elif needs_cast[k]:
            sc = scratch[si]
            si += 1

            @pl.when(pl.program_id(1) == 0)
            def _(sc=sc, w_ref=w_refs[k]):
                sc[...] = w_ref[...].reshape(sc.shape).astype(BF16)

            w_bf.append(sc)
        else:
            w_bf.append(w_refs[k])
    dot = _dot_nt if transposed else _dot
    accs = [dot(a_refs[pairs[k]][...], w_bf[k][...].reshape(w_bf[k].shape[-2:])) for k in range(n_w)]
    o_ref[...] = epilogue(accs, [e[...] for e in e_refs]).astype(o_ref.dtype)


def _matmul(name, layer, a_list, w_list, pairs, epilogue, n_out, out_dtype, tm, tn, extras=(),
            transposed=False, stage_weights=False):
    t = a_list[0].shape[0]
    in_specs, args, needs_cast, staged, scratch = [], [], [], [], []
    for a in a_list:
        in_specs.append(pl.BlockSpec((tm, a.shape[1]), lambda j, i: (i, 0)))
        args.append(a)
    for w, off in w_list:
        at = (layer,) if w.ndim == 3 else ()
        if stage_weights and w.ndim == 3 and w.dtype != BF16:
            blk = (tn, w.shape[-1]) if transposed else (w.shape[-2], tn)
            in_specs.append(pl.BlockSpec(memory_space=pl.ANY))
            scratch += [pltpu.VMEM(blk, BF16), pltpu.VMEM(blk, w.dtype), pltpu.SemaphoreType.DMA(())]
            staged.append(dict(layer=layer, off=off, tn=tn, transposed=transposed))
            needs_cast.append(True)
            args.append(w)
            continue
        staged.append(None)
        if transposed:
            blk = (tn, w.shape[-1])
            spec = pl.BlockSpec(tuple(pl.Element(n) for n in (1,) * len(at) + blk),
                                functools.partial(lambda j, i, o, at: at + (pl.multiple_of(o + j * tn, 32), 0),
                                                  o=off, at=at))
        else:
            blk = (w.shape[-2], tn)
            spec = pl.BlockSpec((None,) * len(at) + blk,
                                functools.partial(lambda j, i, o, at: at + (0, j + o), o=off, at=at))
        in_specs.append(spec)
        if w.dtype != BF16:
            scratch.append(pltpu.VMEM(blk, BF16))
        needs_cast.append(w.dtype != BF16)
        args.append(w)
    for e, spec in extras:
        in_specs.append(spec)
        args.append(e)
    return pl.pallas_call(
        functools.partial(_mm_kernel, n_a=len(a_list), pairs=tuple(pairs), needs_cast=tuple(needs_cast),
                          staged=tuple(staged), n_extra=len(extras), transposed=transposed, epilogue=epilogue),
        grid=(n_out // tn, t // tm),
        in_specs=in_specs,
        out_specs=pl.BlockSpec((tm, tn), lambda j, i: (i, j)),
        out_shape=jax.ShapeDtypeStruct((t, n_out), out_dtype),
        scratch_shapes=scratch,
        compiler_params=_params("arbitrary", "arbitrary"),
        name=name,
    )(*args)


def _tile_spec(tm, tn, off=0):
    return pl.BlockSpec((tm, tn), functools.partial(lambda j, i, o: (i, j + o), o=off))


def _row_spec(tn):
    return pl.BlockSpec((1, tn), lambda j, i: (0, j))


def _rs_spec(tm):
    return pl.BlockSpec((tm, LANES), lambda j, i: (i, 0))


def _lane_tile(rs, n):
    return rs if n == LANES else jnp.concatenate([rs] * (n // LANES), axis=1)


def _ep_scaled(accs, extras):
    return accs[0] * _lane_tile(extras[0], accs[0].shape[1])


def _ep_swiglu(accs, extras):
    rs = _lane_tile(extras[0], accs[0].shape[1])
    return _silu(accs[0] * rs) * (accs[1] * rs)


def _ep_gated_sum(accs, extras):
    gates = [_sigmoid(e.astype(F32)) for e in extras]
    return gates[0] * accs[0] + gates[1] * accs[1] + gates[2] * accs[2]


def _ep_qk_norm_rotary(accs, extras):
    acc = accs[0]
    gain, cosf, sinf, rs = extras
    outs = []
    for c in range(acc.shape[1] // LANES):
        blk = acc[:, c * LANES:(c + 1) * LANES] * rs
        ms = jnp.mean(blk * blk, axis=-1, keepdims=True)
        y = blk * lax.rsqrt(ms + NORM_EPS) * gain[:, c * LANES:(c + 1) * LANES]
        outs.append(y * cosf + pltpu.roll(y, LANES // 2, 1) * sinf)
    return jnp.concatenate(outs, axis=-1)


def _residual_kernel(a_ref, w_ref, x_ref, g_ref, o_ref, xg_ref, rs_ref, wbf_ref, ss_ref, *, scale, d_model):
    j, i = pl.program_id(0), pl.program_id(1)
    tm, tn = o_ref.shape

    @pl.when(i == 0)
    def _():
        wbf_ref[...] = w_ref[...].astype(BF16)

    xn = x_ref[...] + scale * _dot(a_ref[...], wbf_ref[...])
    o_ref[...] = xn
    xg_ref[...] = (xn * g_ref[...]).astype(xg_ref.dtype)
    sq = xn * xn
    part = sq[:, 0:LANES]
    for c in range(1, tn // LANES):
        part = part + sq[:, c * LANES:(c + 1) * LANES]
    part = jnp.broadcast_to(jnp.sum(part, axis=-1, keepdims=True), (tm, LANES))
    rows = pl.ds(pl.multiple_of(i * tm, tm), tm)

    @pl.when(j == 0)
    def _():
        ss_ref[rows, :] = part

    @pl.when(j != 0)
    def _():
        ss_ref[rows, :] = ss_ref[rows, :] + part

    rs_ref[...] = lax.rsqrt(ss_ref[rows, :] * (1.0 / d_model) + NORM_EPS)


def _residual_matmul(name, layer, a, w, x, gain_next, scale, tm, tn):
    t, d = x.shape
    k_dim = w.shape[1]
    x_new, xg, rs_partial = pl.pallas_call(
        functools.partial(_residual_kernel, scale=scale, d_model=d),
        grid=(d // tn, t // tm),
        in_specs=[pl.BlockSpec((tm, k_dim), lambda j, i: (i, 0)),
                  pl.BlockSpec((None, k_dim, tn), lambda j, i: (layer, 0, j)),
                  pl.BlockSpec((tm, tn), lambda j, i: (i, j)),
                  pl.BlockSpec((1, tn), lambda j, i: (0, j))],
        out_specs=[pl.BlockSpec((tm, tn), lambda j, i: (i, j)),
                   pl.BlockSpec((tm, tn), lambda j, i: (i, j)),
                   pl.BlockSpec((None, tm, LANES), lambda j, i: (j, i, 0))],
        out_shape=[jax.ShapeDtypeStruct((t, d), F32), jax.ShapeDtypeStruct((t, d), BF16),
                   jax.ShapeDtypeStruct((d // tn, t, LANES), F32)],
        scratch_shapes=[pltpu.VMEM((k_dim, tn), BF16), pltpu.VMEM((t, LANES), F32)],
        compiler_params=_params("arbitrary", "arbitrary"),
        name=name,
    )(a, w, x, gain_next.reshape(1, d))
    return x_new, xg, rs_partial[d // tn - 1]


def _ffn(layer, x, xg, rs, w_in, w_out, gain_next):
    f = w_out.shape[1]
    tm, tf = 1024, 512
    h = _matmul("ffn_in", layer, [xg], [(w_in, 0), (w_in, f // tf)], (0, 0), _ep_swiglu,
                f, BF16, tm, tf, extras=[(rs, _rs_spec(tm))], stage_weights=True)
    return _residual_matmul("ffn_out", layer, h, w_out, x, gain_next, 0.5, 512, 1024)


def _attn_kernel(q_ref, k_ref, v_ref, lam_ref, gain_ref, linit_ref, o_ref):
    dh = ATTN_HEAD_DIM
    v = v_ref[...]
    lv = lam_ref[...]
    lam_init = linit_ref[...]
    s01 = jnp.sum(lv[0:1] * lv[1:2], axis=-1, keepdims=True)
    s23 = jnp.sum(lv[2:3] * lv[3:4], axis=-1, keepdims=True)
    lam = jnp.exp(s01) - jnp.exp(s23) + lam_init

    def one_map(m):
        s = _dot_nt(q_ref[:, m * dh:(m + 1) * dh], k_ref[:, m * dh:(m + 1) * dh])
        p = jnp.exp2(s - jnp.max(s, axis=-1, keepdims=True))
        inv_l = 1.0 / jnp.sum(p, axis=-1, keepdims=True)
        return _dot(p.astype(BF16), v), inv_l

    o1, r1 = one_map(0)
    o2, r2 = one_map(1)
    o = o1 * r1 - o2 * (lam * r2)
    ms = jnp.mean(o * o, axis=-1, keepdims=True)
    o = o * lax.rsqrt(ms + NORM_EPS) * gain_ref[...] * (1.0 - lam_init)
    o_ref[...] = o.astype(o_ref.dtype)


def _attention(qk, v, lam_vec, sub_gain, lam_init, batch, seq, tq=1024):
    t = qk.shape[0]
    hd = 2 * ATTN_HEAD_DIM
    nq = seq // tq
    return pl.pallas_call(
        _attn_kernel,
        grid=(batch, ATTN_HEADS, nq),
        in_specs=[
            pl.BlockSpec((tq, hd), lambda b, h, i: (b * nq + i, h)),
            pl.BlockSpec((seq, hd), lambda b, h, i: (b, ATTN_HEADS + h)),
            pl.BlockSpec((seq, ATTN_V_DIM), lambda b, h, i: (b, h)),
            pl.BlockSpec((4, ATTN_HEAD_DIM), lambda b, h, i: (0, 0)),
            pl.BlockSpec((1, ATTN_V_DIM), lambda b, h, i: (0, 0)),
            pl.BlockSpec((1, 1), lambda b, h, i: (0, 0)),
        ],
        out_specs=pl.BlockSpec((tq, ATTN_V_DIM), lambda b, h, i: (b * nq + i, h)),
        out_shape=jax.ShapeDtypeStruct((t, ATTN_HEADS * ATTN_V_DIM), BF16),
        compiler_params=_params("parallel", "parallel", "arbitrary"),
        name="diff_attention",
    )(qk, qk, v, lam_vec, sub_gain.reshape(1, -1), lam_init.reshape(1, 1))


def _scan_constants(c, rev):
    idx = np.arange(c)
    if rev:
        cum = idx[None, :] >= idx[:, None]
    else:
        cum = idx[None, :] <= idx[:, None]
    valids = [np.eye(c, dtype=np.float32)]
    s = c // 2
    while s >= 1:
        blk = idx // (2 * s)
        late = (idx // s) % 2
        same = blk[:, None] == blk[None, :]
        if not rev:
            valid = same & (late[:, None] == 1) & (late[None, :] == 0)
        else:
            valid = same & (late[:, None] == 0) & (late[None, :] == 1)
        valids.append(valid.astype(np.float32))
        s //= 2
    return (jnp.asarray(cum.astype(np.float32), dtype=BF16),
            jnp.asarray(np.stack(valids), dtype=F32))


def _level_masks(c, dk, n_lvl, rev):
    row = lax.broadcasted_iota(jnp.int32, (c, dk), 0)
    masks = []
    for lvl in range(1, n_lvl + 1):
        late = ((row >> (n_lvl - lvl)) & 1) == 1
        masks.append(jnp.logical_not(late) if rev else late)
    return masks


def _mid_row(cum, s):
    c, dk = cum.shape
    if s >= 8:
        parts = [jnp.broadcast_to(cum[b * 2 * s + s:b * 2 * s + s + 1, :], (2 * s, dk))
                 for b in range(c // (2 * s))]
        return parts[0] if len(parts) == 1 else jnp.concatenate(parts, axis=0)
    x3 = cum.reshape(c // 8, 8, dk)
    sub = lax.broadcasted_iota(jnp.int32, x3.shape, 1)

    def row(r):
        return jnp.broadcast_to(x3[:, r:r + 1, :], x3.shape)

    if s == 4:
        r3 = row(4)
    elif s == 2:
        r3 = jnp.where(sub < 4, row(2), row(6))
    else:
        r3 = jnp.where(sub < 2, row(1), jnp.where(sub < 4, row(3), jnp.where(sub < 6, row(5), row(7))))
    return r3.reshape(c, dk)


def _scan_chunk(q, k, v, g, st, cum_ref, valid_ref, masks, rev):
    c = q.shape[0]
    n_lvl = len(masks)
    tri = cum_ref[...]
    gh, gm, gl = _split3(g)
    cum = _dot(tri, gh) + _dot(tri, gm) + _dot(tri, gl)
    total = cum[0:1] if rev else cum[c - 1:c]
    a = valid_ref[0] * _dot_nt(q.astype(BF16), k.astype(BF16))
    for lvl in range(1, n_lvl + 1):
        e = -jnp.abs(cum - _mid_row(cum, c >> lvl))
        x = (jnp.where(masks[lvl - 1], q, k) * jnp.exp2(e)).astype(BF16)
        a = a + valid_ref[lvl] * _dot_nt(x, x)
    qd = (q * jnp.exp2(cum)).astype(BF16)
    kd = (k * jnp.exp2(total - cum)).astype(BF16)
    vb = v.astype(BF16)
    o = _dot(a.astype(BF16), vb) + _dot_nt(qd, st.astype(BF16))
    return o, st * jnp.exp2(total) + _dot_tn(vb, kd)


def _gla_gate(lr, w2, bias):
    lr_hi = lr.astype(BF16)
    lr_lo = (lr - lr_hi.astype(F32)).astype(BF16)
    w2_hi = w2.astype(BF16)
    w2_lo = (w2 - w2_hi.astype(F32)).astype(BF16)
    z = _dot(lr_hi, w2_hi) + _dot(lr_hi, w2_lo) + _dot(lr_lo, w2_hi) + bias
    return _log_sigmoid(z) * (LOG2_E / GLA_GATE_NORMALIZER)


def _gla_kernel(qf_ref, kf_ref, vf_ref, lrf_ref, qb_ref, kb_ref, vb_ref, lrb_ref,
                w2f_ref, bf_ref, w2b_ref, bb_ref, dallf_ref, validf_ref, dallb_ref, validb_ref,
                of_ref, ob_ref, stf_ref, stb_ref, *, chunk):
    @pl.when(pl.program_id(2) == 0)
    def _():
        stf_ref[...] = jnp.zeros_like(stf_ref)
        stb_ref[...] = jnp.zeros_like(stb_ref)

    r = GLA_GATE_RANK
    gf_all = _gla_gate(lrf_ref[...][:, 0:r], w2f_ref[...], bf_ref[...])
    gb_all = _gla_gate(lrb_ref[...][:, r:2 * r], w2b_ref[...], bb_ref[...])
    n_lvl = validf_ref.shape[0] - 1
    mf = _level_masks(chunk, GLA_K_DIM, n_lvl, False)
    mb = _level_masks(chunk, GLA_K_DIM, n_lvl, True)
    scale = GLA_K_DIM ** -0.5
    n = qf_ref.shape[0] // chunk
    stf, stb = stf_ref[...], stb_ref[...]
    for ci in range(n):
        sf = slice(ci * chunk, (ci + 1) * chunk)
        sb = slice((n - 1 - ci) * chunk, (n - ci) * chunk)
        o, stf = _scan_chunk(qf_ref[sf, :] * scale, kf_ref[sf, :], vf_ref[sf, :], gf_all[sf, :],
                             stf, dallf_ref, validf_ref, mf, False)
        of_ref[sf, :] = o
        o, stb = _scan_chunk(qb_ref[sb, :] * scale, kb_ref[sb, :], vb_ref[sb, :], gb_all[sb, :],
                             stb, dallb_ref, validb_ref, mb, True)
        ob_ref[sb, :] = o
    stf_ref[...] = stf
    stb_ref[...] = stb


def _hgrn_lower_bound(logits, lmask):
    pe = jnp.exp(logits - jnp.max(logits, axis=0, keepdims=True))
    p = pe / jnp.sum(pe, axis=0, keepdims=True)
    lb = jnp.sum(p * lmask, axis=0, keepdims=True) - p[0:1]
    lb = jnp.clip(lb, 0.0, 1.0 - 1e-6)
    return lb, jnp.log(jnp.maximum(lb, LB_FLOOR)), jnp.log1p(-lb)


def _hgrn_gate(z, lb, log_lb, log_1m):
    t2 = log_1m + _log_sigmoid(z)
    g = jnp.maximum(log_lb, t2) + jnp.log1p(jnp.exp(-jnp.abs(log_lb - t2)))
    return (1.0 - lb) * _sigmoid(-z), g * LOG2_E


def _hgrn_kernel(qf_ref, zf_ref, vf_ref, qb_ref, zb_ref, vb_ref, lbf_ref, lbb_ref, lmask_ref,
                 dallf_ref, validf_ref, dallb_ref, validb_ref, of_ref, ob_ref, stf_ref, stb_ref,
                 *, chunk):
    @pl.when(pl.program_id(2) == 0)
    def _():
        stf_ref[...] = jnp.zeros_like(stf_ref)
        stb_ref[...] = jnp.zeros_like(stb_ref)

    lbf = _hgrn_lower_bound(lbf_ref[...], lmask_ref[...])
    lbb = _hgrn_lower_bound(lbb_ref[...], lmask_ref[...])
    n_lvl = validf_ref.shape[0] - 1
    mf = _level_masks(chunk, HGRN_K_DIM, n_lvl, False)
    mb = _level_masks(chunk, HGRN_K_DIM, n_lvl, True)
    scale = HGRN_K_DIM ** -0.5
    n = qf_ref.shape[0] // chunk
    stf, stb = stf_ref[...], stb_ref[...]
    for ci in range(n):
        sf = slice(ci * chunk, (ci + 1) * chunk)
        sb = slice((n - 1 - ci) * chunk, (n - ci) * chunk)
        k, g = _hgrn_gate(zf_ref[sf, :], *lbf)
        o, stf = _scan_chunk(qf_ref[sf, :] * scale, k, vf_ref[sf, :], g,
                             stf, dallf_ref, validf_ref, mf, False)
        of_ref[sf, :] = o
        k, g = _hgrn_gate(zb_ref[sb, :], *lbb)
        o, stb = _scan_chunk(qb_ref[sb, :] * scale, k, vb_ref[sb, :], g,
                             stb, dallb_ref, validb_ref, mb, True)
        ob_ref[sb, :] = o
    stf_ref[...] = stf
    stb_ref[...] = stb


def _fwd_idx(nblk, col):
    return lambda b, h, c: (b * nblk + c, col(h))


def _bwd_idx(nblk, col):
    return lambda b, h, c: (b * nblk + (nblk - 1 - c), col(h))


def _const_spec(arr):
    zeros = (0,) * arr.ndim
    return pl.BlockSpec(arr.shape, lambda b, h, c: zeros)


def _gla_scan(p_gla, lr, w2f, bf, w2b, bb, batch, seq):
    t = p_gla.shape[0]
    tb, c = SCAN_BLOCK, SCAN_CHUNK
    nblk = seq // tb
    dk, dv, nh = GLA_K_DIM, GLA_V_DIM, GLA_HEADS
    consts = _scan_constants(c, False) + _scan_constants(c, True)
    tok_specs = []
    for idx in (_fwd_idx, _bwd_idx):
        tok_specs += [
            pl.BlockSpec((tb, dk), idx(nblk, lambda h: h)),
            pl.BlockSpec((tb, dk), idx(nblk, lambda h: nh + h)),
            pl.BlockSpec((tb, dv), idx(nblk, lambda h: (2 * nh * dk) // dv + h)),
            pl.BlockSpec((tb, lr.shape[1]), idx(nblk, lambda h: 0)),
        ]
    gate_specs = [pl.BlockSpec((GLA_GATE_RANK, dk), lambda b, h, c_: (0, h)),
                  pl.BlockSpec((1, dk), lambda b, h, c_: (0, h))] * 2
    out_sd = jax.ShapeDtypeStruct((t, nh * dv), F32)
    return pl.pallas_call(
        functools.partial(_gla_kernel, chunk=c),
        grid=(batch, nh, nblk),
        in_specs=tok_specs + gate_specs + [_const_spec(a) for a in consts],
        out_specs=[pl.BlockSpec((tb, dv), _fwd_idx(nblk, lambda h: h)),
                   pl.BlockSpec((tb, dv), _bwd_idx(nblk, lambda h: h))],
        out_shape=[out_sd, out_sd],
        scratch_shapes=[pltpu.VMEM((dv, dk), F32), pltpu.VMEM((dv, dk), F32)],
        compiler_params=_params("parallel", "parallel", "arbitrary"),
        name="gla_scan",
    )(p_gla, p_gla, p_gla, lr, p_gla, p_gla, p_gla, lr,
      w2f, bf.reshape(1, -1), w2b, bb.reshape(1, -1), *consts)


def _hgrn_scan(p_hg, lb_fwd, lb_bwd, lmask, batch, seq):
    t = p_hg.shape[0]
    tb, c = SCAN_BLOCK, SCAN_CHUNK
    nblk = seq // tb
    dk, dv, nh = HGRN_K_DIM, HGRN_V_DIM, HGRN_HEADS
    consts = _scan_constants(c, False) + _scan_constants(c, True)
    n_layers = lb_fwd.shape[0]
    tok_specs = []
    for idx, z_col in ((_fwd_idx, nh), (_bwd_idx, 2 * nh)):
        tok_specs += [
            pl.BlockSpec((tb, dk), idx(nblk, lambda h: h)),
            pl.BlockSpec((tb, dk), idx(nblk, functools.partial(lambda h, z: z + h, z=z_col))),
            pl.BlockSpec((tb, dv), idx(nblk, lambda h: (3 * nh * dk) // dv + h)),
        ]
    lb_specs = [pl.BlockSpec((n_layers, dk), lambda b, h, c_: (0, h)),
                pl.BlockSpec((n_layers, dk), lambda b, h, c_: (0, h)),
                pl.BlockSpec((n_layers, 1), lambda b, h, c_: (0, 0))]
    out_sd = jax.ShapeDtypeStruct((t, nh * dv), F32)
    return pl.pallas_call(
        functools.partial(_hgrn_kernel, chunk=c),
        grid=(batch, nh, nblk),
        in_specs=tok_specs + lb_specs + [_const_spec(a) for a in consts],
        out_specs=[pl.BlockSpec((tb, dv), _fwd_idx(nblk, lambda h: h)),
                   pl.BlockSpec((tb, dv), _bwd_idx(nblk, lambda h: h))],
        out_shape=[out_sd, out_sd],
        scratch_shapes=[pltpu.VMEM((dv, dk), F32), pltpu.VMEM((dv, dk), F32)],
        compiler_params=_params("parallel", "parallel", "arbitrary"),
        name="hgrn_scan",
    )(p_hg, p_hg, p_hg, p_hg, p_hg, p_hg, lb_fwd, lb_bwd, lmask, *consts)


def _headnorm_gate_kernel(of_ref, ob_ref, r_ref, gain_ref, o_ref, *, hd):
    o = of_ref[...] + ob_ref[...]
    gain = gain_ref[...]
    outs = []
    for h in range(o.shape[1] // hd):
        blk = o[:, h * hd:(h + 1) * hd]
        ms = jnp.mean(blk * blk, axis=-1, keepdims=True)
        outs.append(blk * lax.rsqrt(ms + NORM_EPS) * gain)
    y = jnp.concatenate(outs, axis=-1) * _silu(r_ref[...])
    o_ref[...] = y.astype(o_ref.dtype)


def _headnorm_gate(o_f, o_b, p, r_off, gain, hd, tm=512):
    t, w = o_f.shape
    return pl.pallas_call(
        functools.partial(_headnorm_gate_kernel, hd=hd),
        grid=(t // tm,),
        in_specs=[pl.BlockSpec((tm, w), lambda i: (i, 0)),
                  pl.BlockSpec((tm, w), lambda i: (i, 0)),
                  pl.BlockSpec((tm, w), lambda i: (i, r_off)),
                  pl.BlockSpec((1, hd), lambda i: (0, 0))],
        out_specs=pl.BlockSpec((tm, w), lambda i: (i, 0)),
        out_shape=jax.ShapeDtypeStruct((t, w), BF16),
        compiler_params=_params("parallel"),
        name="headnorm_gate",
    )(o_f, o_b, p, gain.reshape(1, hd))


def _rotary_tables(seq):
    half = ATTN_HEAD_DIM // 2
    inv_freq = ROPE_THETA ** (-jnp.arange(half, dtype=F32) / half)
    ang = jnp.arange(seq, dtype=jnp.int32).astype(F32)[:, None] * inv_freq[None, :]
    cos, sin = jnp.cos(ang), jnp.sin(ang)
    return jnp.concatenate([cos, cos], axis=-1), jnp.concatenate([-sin, sin], axis=-1)


def kernel(x, ffn1_norm, ffn1_w_in, ffn1_w_out, mix_norm, w_in, attn_q_norm, attn_k_norm, attn_lambda, attn_sub_norm, gla_gate_w2_fwd, gla_gate_b_fwd, gla_gate_w2_bwd, gla_gate_b_bwd, gla_out_norm, hgrn_lb_fwd, hgrn_lb_bwd, hgrn_out_norm, w_branch_attn, w_branch_gla, w_branch_hgrn, w_out, ffn2_norm, ffn2_w_in, ffn2_w_out):
    batch, seq, d = x.shape
    t = batch * seq
    depth = w_in.shape[0]
    cosf, sinf = _rotary_tables(seq)

    qk_w = 2 * ATTN_HEADS * 2 * ATTN_HEAD_DIM
    av_w = ATTN_HEADS * ATTN_V_DIM
    gla_w = 2 * GLA_HEADS * GLA_K_DIM + 2 * GLA_HEADS * GLA_V_DIM
    lr_w = 2 * GLA_GATE_RANK
    hg_w = 3 * HGRN_HEADS * HGRN_K_DIM + 2 * HGRN_HEADS * HGRN_V_DIM
    c_av, c_gla = qk_w, qk_w + av_w
    c_lr = c_gla + gla_w
    c_tail = c_lr + lr_w

    lmasks = np.tril(np.ones((depth, depth), np.float32))[:, :, None]

    tm, tn = 1024, 512
    tm_p, tn_p = 1024, 1024
    tm_w, tn_w = 512, 2048
    q_scale = ATTN_HEAD_DIM ** -0.5 * LOG2_E
    n_sub = qk_w // (2 * ATTN_HEAD_DIM)
    nrow = seq // tm_w
    rot_spec = pl.BlockSpec((tm_w, LANES), lambda j, i: (i % nrow, 0))

    w_in_t = jnp.swapaxes(w_in, 1, 2)
    w_qk_t = w_in_t[:, :qk_w, :].astype(BF16)

    xc = x.reshape(t, d)
    xg, rs = _rmsnorm(xc, ffn1_norm[0])
    for l in range(depth):
        lam_init = jnp.full((1, 1), 0.8 - 0.6 * math.exp(-0.3 * l), F32)
        xc, hn, hrs = _ffn(l, xc, xg, rs, ffn1_w_in, ffn1_w_out, mix_norm[l])

        qk_gain = jnp.concatenate([jnp.tile(attn_q_norm[l] * q_scale, n_sub),
                                   jnp.tile(attn_k_norm[l], n_sub)]).reshape(1, qk_w)
        qk = _matmul("proj_qk", l, [hn], [(w_qk_t, 0)], (0,), _ep_qk_norm_rotary,
                     qk_w, BF16, tm_w, tn_w, transposed=True,
                     extras=[(qk_gain, _row_spec(tn_w)), (cosf, rot_spec), (sinf, rot_spec),
                             (hrs, _rs_spec(tm_w))])
        proj = functools.partial(_matmul, layer=l, a_list=[hn], pairs=(0,), epilogue=_ep_scaled,
                                 tm=tm_p, tn=tn_p, transposed=True, stage_weights=True,
                                 extras=[(hrs, _rs_spec(tm_p))])
        p_av = proj("proj_av", w_list=[(w_in_t, c_av)], n_out=av_w, out_dtype=BF16)
        p_gla = proj("proj_gla", w_list=[(w_in_t, c_gla)], n_out=gla_w, out_dtype=F32)
        p_lr = proj("proj_lr", w_list=[(w_in_t, c_lr)], n_out=LANES, out_dtype=F32, tn=LANES)
        p_hg = proj("proj_hgrn", w_list=[(w_in_t, c_tail)], n_out=hg_w, out_dtype=F32)
        p_gate = proj("proj_gate", w_list=[(w_in_t, c_tail + hg_w)], n_out=3 * d, out_dtype=BF16)

        o_a = _attention(qk, p_av, attn_lambda[l], attn_sub_norm[l], lam_init, batch, seq)

        og_f, og_b = _gla_scan(p_gla, p_lr, gla_gate_w2_fwd[l], gla_gate_b_fwd[l],
                               gla_gate_w2_bwd[l], gla_gate_b_bwd[l], batch, seq)
        u_g = _headnorm_gate(og_f, og_b, p_gla, gla_w // (GLA_HEADS * GLA_V_DIM) - 1, gla_out_norm[l],
                             GLA_V_DIM)

        oh_f, oh_b = _hgrn_scan(p_hg, hgrn_lb_fwd, hgrn_lb_bwd, jnp.asarray(lmasks[l]), batch, seq)
        u_h = _headnorm_gate(oh_f, oh_b, p_hg, hg_w // (HGRN_HEADS * HGRN_V_DIM) - 1, hgrn_out_norm[l],
                             HGRN_V_DIM)

        merged = _matmul(
            "merge", l, [o_a, u_g, u_h],
            [(w_branch_attn, 0), (w_branch_gla, 0), (w_branch_hgrn, 0)], (0, 1, 2),
            _ep_gated_sum, d, BF16, tm, tn,
            extras=[(p_gate, _tile_spec(tm, tn, b * d // tn)) for b in range(3)])
        xc, xg, rs = _residual_matmul("out_proj", l, merged, w_out, xc, ffn2_norm[l], 1.0, tm, tn)

        xc, xg, rs = _ffn(l, xc, xg, rs, ffn2_w_in, ffn2_w_out, ffn1_norm[(l + 1) % depth])
    return xc.reshape(batch, seq, d)
```

```python
import functools
import math

import numpy as np
import jax
import jax.numpy as jnp
from jax import lax
from jax.experimental import pallas as pl
from jax.experimental.pallas import tpu as pltpu

F32 = jnp.float32
BF16 = jnp.bfloat16

NORM_EPS = 1e-6
LB_FLOOR = 1e-20
ROPE_THETA = 10000.0
GLA_GATE_NORMALIZER = 16.0

ATTN_HEADS = 8
ATTN_HEAD_DIM = 128
ATTN_V_DIM = 256
GLA_HEADS = 4
GLA_K_DIM = 128
GLA_V_DIM = 256
GLA_GATE_RANK = 16
HGRN_HEADS = 8
HGRN_K_DIM = 128
HGRN_V_DIM = 128

LANES = 128
VMEM_LIMIT_BYTES = 58 * 2**20
SCAN_CHUNK = 128
SCAN_BLOCK = 512
ATTN_ROW_GROUPS = 2
LOG2_E = math.log2(math.e)


def _params(*sem):
    return pltpu.CompilerParams(dimension_semantics=sem, vmem_limit_bytes=VMEM_LIMIT_BYTES)


def _dot(a, b):
    return jnp.dot(a, b, preferred_element_type=F32)


def _dot_nt(a, b):
    return lax.dot_general(a, b, (((1,), (1,)), ((), ())), preferred_element_type=F32)


def _dot_tn(a, b):
    return lax.dot_general(a, b, (((0,), (0,)), ((), ())), preferred_element_type=F32)


def _sigmoid(x):
    return 1.0 / (1.0 + jnp.exp(-x))


def _silu(x):
    return x * _sigmoid(x)


def _log_sigmoid(x):
    return jnp.minimum(x, 0.0) - jnp.log1p(jnp.exp(-jnp.abs(x)))


def _split3(x):
    hi = x.astype(BF16)
    r1 = x - hi.astype(F32)
    mid = r1.astype(BF16)
    lo = (r1 - mid.astype(F32)).astype(BF16)
    return hi, mid, lo


def _rmsnorm_kernel(x_ref, g_ref, o_ref, rs_ref):
    x = x_ref[...]
    ms = jnp.mean(x * x, axis=-1, keepdims=True)
    o_ref[...] = (x * g_ref[...]).astype(o_ref.dtype)
    rs_ref[...] = jnp.broadcast_to(lax.rsqrt(ms + NORM_EPS), rs_ref.shape)


def _rmsnorm(x, gain, tm=256):
    t, d = x.shape
    return pl.pallas_call(
        _rmsnorm_kernel,
        grid=(t // tm,),
        in_specs=[pl.BlockSpec((tm, d), lambda i: (i, 0)),
                  pl.BlockSpec((1, d), lambda i: (0, 0))],
        out_specs=[pl.BlockSpec((tm, d), lambda i: (i, 0)),
                   pl.BlockSpec((tm, LANES), lambda i: (i, 0))],
        out_shape=[jax.ShapeDtypeStruct((t, d), BF16), jax.ShapeDtypeStruct((t, LANES), F32)],
        compiler_params=_params("parallel"),
        name="rmsnorm",
    )(x, gain.reshape(1, d))


def _weight_block_copy(w_hbm, stage, sem, j, *, layer, off, tn, transposed):
    if transposed:
        src = w_hbm.at[layer, pl.ds(pl.multiple_of(off + j * tn, 32), tn), :]
    else:
        src = w_hbm.at[layer, :, pl.ds(pl.multiple_of((j + off) * tn, LANES), tn)]
    return pltpu.make_async_copy(src, stage, sem)


def _mm_kernel(*refs, n_a, pairs, needs_cast, staged, n_extra, transposed, epilogue):
    n_w = len(pairs)
    a_refs = refs[:n_a]
    w_refs = refs[n_a:n_a + n_w]
    e_refs = refs[n_a + n_w:n_a + n_w + n_extra]
    o_ref = refs[n_a + n_w + n_extra]
    scratch = refs[n_a + n_w + n_extra + 1:]
    w_bf = []
    si = 0
    for k in range(n_w):
        if staged[k] is not None:
            sc, stage, sem = scratch[si:si + 3]
            si += 3
            copy = functools.partial(_weight_block_copy, w_refs[k], stage, sem, **staged[k])

            @pl.when(pl.program_id(1) == 0)
            def _(sc=sc, stage=stage, copy=copy):
                j = pl.program_id(0)

                @pl.when(j == 0)
                def _():
                    copy(j).start()

                copy(j).wait()
                sc[...] = stage[...].astype(BF16)

                @pl.when(j + 1 < pl.num_programs(0))
                def _():
                    copy(j + 1).start()

            w_bf.append(sc)
        elif needs_cast[k]:
            sc = scratch[si]
            si += 1

            @pl.when(pl.program_id(1) == 0)
            def _(sc=sc, w_ref=w_refs[k]):
                sc[...] = w_ref[...].reshape(sc.shape).astype(BF16)

            w_bf.append(sc)
        else:
            w_bf.append(w_refs[k])
    dot = _dot_nt if transposed else _dot
    accs = [dot(a_refs[pairs[k]][...], w_bf[k][...].reshape(w_bf[k].shape[-2:])) for k in range(n_w)]
    o_ref[...] = epilogue(accs, [e[...] for e in e_refs]).astype(o_ref.dtype)


def _matmul(name, layer, a_list, w_list, pairs, epilogue, n_out, out_dtype, tm, tn, extras=(),
            transposed=False, stage_weights=False):
    t = a_list[0].shape[0]
    in_specs, args, needs_cast, staged, scratch = [], [], [], [], []
    for a in a_list:
        in_specs.append(pl.BlockSpec((tm, a.shape[1]), lambda j, i: (i, 0)))
        args.append(a)
    for w, off in w_list:
        at = (layer,) if w.ndim == 3 else ()
        if stage_weights and w.ndim == 3 and w.dtype != BF16:
            blk = (tn, w.shape[-1]) if transposed else (w.shape[-2], tn)
            in_specs.append(pl.BlockSpec(memory_space=pl.ANY))
            scratch += [pltpu.VMEM(blk, BF16), pltpu.VMEM(blk, w.dtype), pltpu.SemaphoreType.DMA(())]
            staged.append(dict(layer=layer, off=off, tn=tn, transposed=transposed))
            needs_cast.append(True)
            args.append(w)
            continue
        staged.append(None)
        if transposed:
            blk = (tn, w.shape[-1])
            spec = pl.BlockSpec(tuple(pl.Element(n) for n in (1,) * len(at) + blk),
                                functools.partial(lambda j, i, o, at: at + (pl.multiple_of(o + j * tn, 32), 0),
                                                  o=off, at=at))
        else:
            blk = (w.shape[-2], tn)
            spec = pl.BlockSpec((None,) * len(at) + blk,
                                functools.partial(lambda j, i, o, at: at + (0, j + o), o=off, at=at))
        in_specs.append(spec)
        if w.dtype != BF16:
            scratch.append(pltpu.VMEM(blk, BF16))
        needs_cast.append(w.dtype != BF16)
        args.append(w)
    for e, spec in extras:
        in_specs.append(spec)
        args.append(e)
    return pl.pallas_call(
        functools.partial(_mm_kernel, n_a=len(a_list), pairs=tuple(pairs), needs_cast=tuple(needs_cast),
                          staged=tuple(staged), n_extra=len(extras), transposed=transposed, epilogue=epilogue),
        grid=(n_out // tn, t // tm),
        in_specs=in_specs,
        out_specs=pl.BlockSpec((tm, tn), lambda j, i: (i, j)),
        out_shape=jax.ShapeDtypeStruct((t, n_out), out_dtype),
        scratch_shapes=scratch,
        compiler_params=_params("arbitrary", "arbitrary"),
        name=name,
    )(*args)


def _tile_spec(tm, tn, off=0):
    return pl.BlockSpec((tm, tn), functools.partial(lambda j, i, o: (i, j + o), o=off))


def _row_spec(tn):
    return pl.BlockSpec((1, tn), lambda j, i: (0, j))


def _rs_spec(tm):
    return pl.BlockSpec((tm, LANES), lambda j, i: (i, 0))


def _lane_tile(rs, n):
    return rs if n == LANES else jnp.concatenate([rs] * (n // LANES), axis=1)


def _ep_scaled(accs, extras):
    return accs[0] * _lane_tile(extras[0], accs[0].shape[1])


def _ep_swiglu(accs, extras):
    rs = _lane_tile(extras[0], accs[0].shape[1])
    return _silu(accs[0] * rs) * (accs[1] * rs)


def _ep_gated_sum(accs, extras):
    gates = [_sigmoid(e.astype(F32)) for e in extras]
    return gates[0] * accs[0] + gates[1] * accs[1] + gates[2] * accs[2]


def _ep_qk_norm_rotary(accs, extras):
    acc = accs[0]
    gain, cosf, sinf, rs = extras
    outs = []
    for c in range(acc.shape[1] // LANES):
        blk = acc[:, c * LANES:(c + 1) * LANES] * rs
        ms = jnp.mean(blk * blk, axis=-1, keepdims=True)
        y = blk * lax.rsqrt(ms + NORM_EPS) * gain[:, c * LANES:(c + 1) * LANES]
        outs.append(y * cosf + pltpu.roll(y, LANES // 2, 1) * sinf)
    return jnp.concatenate(outs, axis=-1)


def _residual_kernel(a_ref, w_ref, x_ref, g_ref, o_ref, xg_ref, rs_ref, wbf_ref, ss_ref, *, scale, d_model):
    j, i = pl.program_id(0), pl.program_id(1)
    tm, tn = o_ref.shape

    @pl.when(i == 0)
    def _():
        wbf_ref[...] = w_ref[...].astype(BF16)

    xn = x_ref[...] + scale * _dot(a_ref[...], wbf_ref[...])
    o_ref[...] = xn
    xg_ref[...] = (xn * g_ref[...]).astype(xg_ref.dtype)
    sq = xn * xn
    part = sq[:, 0:LANES]
    for c in range(1, tn // LANES):
        part = part + sq[:, c * LANES:(c + 1) * LANES]
    part = jnp.broadcast_to(jnp.sum(part, axis=-1, keepdims=True), (tm, LANES))
    rows = pl.ds(pl.multiple_of(i * tm, tm), tm)

    @pl.when(j == 0)
    def _():
        ss_ref[rows, :] = part

    @pl.when(j != 0)
    def _():
        ss_ref[rows, :] = ss_ref[rows, :] + part

    rs_ref[...] = lax.rsqrt(ss_ref[rows, :] * (1.0 / d_model) + NORM_EPS)


def _residual_matmul(name, layer, a, w, x, gain_next, scale, tm, tn):
    t, d = x.shape
    k_dim = w.shape[1]
    x_new, xg, rs_partial = pl.pallas_call(
        functools.partial(_residual_kernel, scale=scale, d_model=d),
        grid=(d // tn, t // tm),
        in_specs=[pl.BlockSpec((tm, k_dim), lambda j, i: (i, 0)),
                  pl.BlockSpec((None, k_dim, tn), lambda j, i: (layer, 0, j)),
                  pl.BlockSpec((tm, tn), lambda j, i: (i, j)),
                  pl.BlockSpec((1, tn), lambda j, i: (0, j))],
        out_specs=[pl.BlockSpec((tm, tn), lambda j, i: (i, j)),
                   pl.BlockSpec((tm, tn), lambda j, i: (i, j)),
                   pl.BlockSpec((None, tm, LANES), lambda j, i: (j, i, 0))],
        out_shape=[jax.ShapeDtypeStruct((t, d), F32), jax.ShapeDtypeStruct((t, d), BF16),
                   jax.ShapeDtypeStruct((d // tn, t, LANES), F32)],
        scratch_shapes=[pltpu.VMEM((k_dim, tn), BF16), pltpu.VMEM((t, LANES), F32)],
        compiler_params=_params("arbitrary", "arbitrary"),
        name=name,
    )(a, w, x, gain_next.reshape(1, d))
    return x_new, xg, rs_partial[d // tn - 1]


def _ffn(layer, x, xg, rs, w_in, w_out, gain_next):
    f = w_out.shape[1]
    tm, tf = 1024, 512
    h = _matmul("ffn_in", layer, [xg], [(w_in, 0), (w_in, f // tf)], (0, 0), _ep_swiglu,
                f, BF16, tm, tf, extras=[(rs, _rs_spec(tm))], stage_weights=True)
    return _residual_matmul("ffn_out", layer, h, w_out, x, gain_next, 0.5, 512, 1024)


def _attn_kernel(q_ref, k_ref, v_ref, lam_ref, gain_ref, linit_ref, o_ref):
    dh = ATTN_HEAD_DIM
    v = v_ref[...]
    lv = lam_ref[...]
    lam_init = linit_ref[...]
    s01 = jnp.sum(lv[0:1] * lv[1:2], axis=-1, keepdims=True)
    s23 = jnp.sum(lv[2:3] * lv[3:4], axis=-1, keepdims=True)
    lam = jnp.exp(s01) - jnp.exp(s23) + lam_init

    def scores(rows):
        return [_dot_nt(q_ref[rows, m * dh:(m + 1) * dh], k_ref[:, m * dh:(m + 1) * dh]) for m in range(2)]

    def softmax(s):
        p = [jnp.exp2(s[m] - jnp.max(s[m], axis=-1, keepdims=True)) for m in range(2)]
        return p, [1.0 / jnp.sum(p[m], axis=-1, keepdims=True) for m in range(2)]

    def values(p, r):
        o1, o2 = [_dot(p[m].astype(BF16), v) for m in range(2)]
        return o1 * r[0] - o2 * (lam * r[1])

    rows = q_ref.shape[0] // ATTN_ROW_GROUPS
    groups = [slice(g * rows, (g + 1) * rows) for g in range(ATTN_ROW_GROUPS)]
    s_next = scores(groups[0])
    outs = []
    for g in range(ATTN_ROW_GROUPS):
        s_cur = s_next
        if g + 1 < ATTN_ROW_GROUPS:
            s_next = scores(groups[g + 1])
        outs.append(values(*softmax(s_cur)))
    o = jnp.concatenate(outs, axis=0)
    ms = jnp.mean(o * o, axis=-1, keepdims=True)
    o = o * lax.rsqrt(ms + NORM_EPS) * gain_ref[...] * (1.0 - lam_init)
    o_ref[...] = o.astype(o_ref.dtype)


def _attention(qk, v, lam_vec, sub_gain, lam_init, batch, seq, tq=1024):
    t = qk.shape[0]
    hd = 2 * ATTN_HEAD_DIM
    nq = seq // tq
    return pl.pallas_call(
        _attn_kernel,
        grid=(batch, ATTN_HEADS, nq),
        in_specs=[
            pl.BlockSpec((tq, hd), lambda b, h, i: (b * nq + i, h)),
            pl.BlockSpec((seq, hd), lambda b, h, i: (b, ATTN_HEADS + h)),
            pl.BlockSpec((seq, ATTN_V_DIM), lambda b, h, i: (b, h)),
            pl.BlockSpec((4, ATTN_HEAD_DIM), lambda b, h, i: (0, 0)),
            pl.BlockSpec((1, ATTN_V_DIM), lambda b, h, i: (0, 0)),
            pl.BlockSpec((1, 1), lambda b, h, i: (0, 0)),
        ],
        out_specs=pl.BlockSpec((tq, ATTN_V_DIM), lambda b, h, i: (b * nq + i, h)),
        out_shape=jax.ShapeDtypeStruct((t, ATTN_HEADS * ATTN_V_DIM), BF16),
        compiler_params=_params("parallel", "parallel", "arbitrary"),
        name="diff_attention",
    )(qk, qk, v, lam_vec, sub_gain.reshape(1, -1), lam_init.reshape(1, 1))


def _scan_constants(c, rev):
    idx = np.arange(c)
    if rev:
        cum = idx[None, :] >= idx[:, None]
    else:
        cum = idx[None, :] <= idx[:, None]
    valids = [np.eye(c, dtype=np.float32)]
    s = c // 2
    while s >= 1:
        blk = idx // (2 * s)
        late = (idx // s) % 2
        same = blk[:, None] == blk[None, :]
        if not rev:
            valid = same & (late[:, None] == 1) & (late[None, :] == 0)
        else:
            valid = same & (late[:, None] == 0) & (late[None, :] == 1)
        valids.append(valid.astype(np.float32))
        s //= 2
    return (jnp.asarray(cum.astype(np.float32), dtype=BF16),
            jnp.asarray(np.stack(valids), dtype=F32))


def _level_masks(c, dk, n_lvl, rev):
    row = lax.broadcasted_iota(jnp.int32, (c, dk), 0)
    masks = []
    for lvl in range(1, n_lvl + 1):
        late = ((row >> (n_lvl - lvl)) & 1) == 1
        masks.append(jnp.logical_not(late) if rev else late)
    return masks


def _mid_row(cum, s):
    c, dk = cum.shape
    if s >= 8:
        parts = [jnp.broadcast_to(cum[b * 2 * s + s:b * 2 * s + s + 1, :], (2 * s, dk))
                 for b in range(c // (2 * s))]
        return parts[0] if len(parts) == 1 else jnp.concatenate(parts, axis=0)
    x3 = cum.reshape(c // 8, 8, dk)
    sub = lax.broadcasted_iota(jnp.int32, x3.shape, 1)

    def row(r):
        return jnp.broadcast_to(x3[:, r:r + 1, :], x3.shape)

    if s == 4:
        r3 = row(4)
    elif s == 2:
        r3 = jnp.where(sub < 4, row(2), row(6))
    else:
        r3 = jnp.where(sub < 2, row(1), jnp.where(sub < 4, row(3), jnp.where(sub < 6, row(5), row(7))))
    return r3.reshape(c, dk)


def _scan_chunk(q, k, v, g, st, cum_ref, valid_ref, masks, rev):
    c = q.shape[0]
    n_lvl = len(masks)
    tri = cum_ref[...]
    gh, gm, gl = _split3(g)
    cum = _dot(tri, gh) + _dot(tri, gm) + _dot(tri, gl)
    total = cum[0:1] if rev else cum[c - 1:c]
    a = valid_ref[0] * _dot_nt(q.astype(BF16), k.astype(BF16))
    for lvl in range(1, n_lvl + 1):
        e = -jnp.abs(cum - _mid_row(cum, c >> lvl))
        x = (jnp.where(masks[lvl - 1], q, k) * jnp.exp2(e)).astype(BF16)
        a = a + valid_ref[lvl] * _dot_nt(x, x)
    qd = (q * jnp.exp2(cum)).astype(BF16)
    kd = (k * jnp.exp2(total - cum)).astype(BF16)
    vb = v.astype(BF16)
    o = _dot(a.astype(BF16), vb) + _dot_nt(qd, st.astype(BF16))
    return o, st * jnp.exp2(total) + _dot_tn(vb, kd)


def _gla_gate(lr, w2, bias):
    lr_hi = lr.astype(BF16)
    lr_lo = (lr - lr_hi.astype(F32)).astype(BF16)
    w2_hi = w2.astype(BF16)
    w2_lo = (w2 - w2_hi.astype(F32)).astype(BF16)
    z = _dot(lr_hi, w2_hi) + _dot(lr_hi, w2_lo) + _dot(lr_lo, w2_hi) + bias
    return _log_sigmoid(z) * (LOG2_E / GLA_GATE_NORMALIZER)


def _gla_kernel(qf_ref, kf_ref, vf_ref, lrf_ref, qb_ref, kb_ref, vb_ref, lrb_ref,
                w2f_ref, bf_ref, w2b_ref, bb_ref, dallf_ref, validf_ref, dallb_ref, validb_ref,
                of_ref, ob_ref, stf_ref, stb_ref, *, chunk):
    @pl.when(pl.program_id(2) == 0)
    def _():
        stf_ref[...] = jnp.zeros_like(stf_ref)
        stb_ref[...] = jnp.zeros_like(stb_ref)

    r = GLA_GATE_RANK
    gf_all = _gla_gate(lrf_ref[...][:, 0:r], w2f_ref[...], bf_ref[...])
    gb_all = _gla_gate(lrb_ref[...][:, r:2 * r], w2b_ref[...], bb_ref[...])
    n_lvl = validf_ref.shape[0] - 1
    mf = _level_masks(chunk, GLA_K_DIM, n_lvl, False)
    mb = _level_masks(chunk, GLA_K_DIM, n_lvl, True)
    scale = GLA_K_DIM ** -0.5
    n = qf_ref.shape[0] // chunk
    stf, stb = stf_ref[...], stb_ref[...]
    for ci in range(n):
        sf = slice(ci * chunk, (ci + 1) * chunk)
        sb = slice((n - 1 - ci) * chunk, (n - ci) * chunk)
        o, stf = _scan_chunk(qf_ref[sf, :] * scale, kf_ref[sf, :], vf_ref[sf, :], gf_all[sf, :],
                             stf, dallf_ref, validf_ref, mf, False)
        of_ref[sf, :] = o
        o, stb = _scan_chunk(qb_ref[sb, :] * scale, kb_ref[sb, :], vb_ref[sb, :], gb_all[sb, :],
                             stb, dallb_ref, validb_ref, mb, True)
        ob_ref[sb, :] = o
    stf_ref[...] = stf
    stb_ref[...] = stb


def _hgrn_lower_bound(logits, lmask):
    pe = jnp.exp(logits - jnp.max(logits, axis=0, keepdims=True))
    p = pe / jnp.sum(pe, axis=0, keepdims=True)
    lb = jnp.sum(p * lmask, axis=0, keepdims=True) - p[0:1]
    lb = jnp.clip(lb, 0.0, 1.0 - 1e-6)
    return lb, jnp.log(jnp.maximum(lb, LB_FLOOR)), jnp.log1p(-lb)


def _hgrn_gate(z, lb, log_lb, log_1m):
    t2 = log_1m + _log_sigmoid(z)
    g = jnp.maximum(log_lb, t2) + jnp.log1p(jnp.exp(-jnp.abs(log_lb - t2)))
    return (1.0 - lb) * _sigmoid(-z), g * LOG2_E


def _hgrn_kernel(qf_ref, zf_ref, vf_ref, qb_ref, zb_ref, vb_ref, lbf_ref, lbb_ref, lmask_ref,
                 dallf_ref, validf_ref, dallb_ref, validb_ref, of_ref, ob_ref, stf_ref, stb_ref,
                 *, chunk):
    @pl.when(pl.program_id(2) == 0)
    def _():
        stf_ref[...] = jnp.zeros_like(stf_ref)
        stb_ref[...] = jnp.zeros_like(stb_ref)

    lbf = _hgrn_lower_bound(lbf_ref[...], lmask_ref[...])
    lbb = _hgrn_lower_bound(lbb_ref[...], lmask_ref[...])
    n_lvl = validf_ref.shape[0] - 1
    mf = _level_masks(chunk, HGRN_K_DIM, n_lvl, False)
    mb = _level_masks(chunk, HGRN_K_DIM, n_lvl, True)
    scale = HGRN_K_DIM ** -0.5
    n = qf_ref.shape[0] // chunk
    stf, stb = stf_ref[...], stb_ref[...]
    for ci in range(n):
        sf = slice(ci * chunk, (ci + 1) * chunk)
        sb = slice((n - 1 - ci) * chunk, (n - ci) * chunk)
        k, g = _hgrn_gate(zf_ref[sf, :], *lbf)
        o, stf = _scan_chunk(qf_ref[sf, :] * scale, k, vf_ref[sf, :], g,
                             stf, dallf_ref, validf_ref, mf, False)
        of_ref[sf, :] = o
        k, g = _hgrn_gate(zb_ref[sb, :], *lbb)
        o, stb = _scan_chunk(qb_ref[sb, :] * scale, k, vb_ref[sb, :], g,
                             stb, dallb_ref, validb_ref, mb, True)
        ob_ref[sb, :] = o
    stf_ref[...] = stf
    stb_ref[...] = stb


def _fwd_idx(nblk, col):
    return lambda b, h, c: (b * nblk + c, col(h))


def _bwd_idx(nblk, col):
    return lambda b, h, c: (b * nblk + (nblk - 1 - c), col(h))


def _const_spec(arr):
    zeros = (0,) * arr.ndim
    return pl.BlockSpec(arr.shape, lambda b, h, c: zeros)


def _gla_scan(p_gla, lr, w2f, bf, w2b, bb, batch, seq):
    t = p_gla.shape[0]
    tb, c = SCAN_BLOCK, SCAN_CHUNK
    nblk = seq // tb
    dk, dv, nh = GLA_K_DIM, GLA_V_DIM, GLA_HEADS
    consts = _scan_constants(c, False) + _scan_constants(c, True)
    tok_specs = []
    for idx in (_fwd_idx, _bwd_idx):
        tok_specs += [
            pl.BlockSpec((tb, dk), idx(nblk, lambda h: h)),
            pl.BlockSpec((tb, dk), idx(nblk, lambda h: nh + h)),
            pl.BlockSpec((tb, dv), idx(nblk, lambda h: (2 * nh * dk) // dv + h)),
            pl.BlockSpec((tb, lr.shape[1]), idx(nblk, lambda h: 0)),
        ]
    gate_specs = [pl.BlockSpec((GLA_GATE_RANK, dk), lambda b, h, c_: (0, h)),
                  pl.BlockSpec((1, dk), lambda b, h, c_: (0, h))] * 2
    out_sd = jax.ShapeDtypeStruct((t, nh * dv), F32)
    return pl.pallas_call(
        functools.partial(_gla_kernel, chunk=c),
        grid=(batch, nh, nblk),
        in_specs=tok_specs + gate_specs + [_const_spec(a) for a in consts],
        out_specs=[pl.BlockSpec((tb, dv), _fwd_idx(nblk, lambda h: h)),
                   pl.BlockSpec((tb, dv), _bwd_idx(nblk, lambda h: h))],
        out_shape=[out_sd, out_sd],
        scratch_shapes=[pltpu.VMEM((dv, dk), F32), pltpu.VMEM((dv, dk), F32)],
        compiler_params=_params("parallel", "parallel", "arbitrary"),
        name="gla_scan",
    )(p_gla, p_gla, p_gla, lr, p_gla, p_gla, p_gla, lr,
      w2f, bf.reshape(1, -1), w2b, bb.reshape(1, -1), *consts)


def _hgrn_scan(p_hg, lb_fwd, lb_bwd, lmask, batch, seq):
    t = p_hg.shape[0]
    tb, c = SCAN_BLOCK, SCAN_CHUNK
    nblk = seq // tb
    dk, dv, nh = HGRN_K_DIM, HGRN_V_DIM, HGRN_HEADS
    consts = _scan_constants(c, False) + _scan_constants(c, True)
    n_layers = lb_fwd.shape[0]
    tok_specs = []
    for idx, z_col in ((_fwd_idx, nh), (_bwd_idx, 2 * nh)):
        tok_specs += [
            pl.BlockSpec((tb, dk), idx(nblk, lambda h: h)),
            pl.BlockSpec((tb, dk), idx(nblk, functools.partial(lambda h, z: z + h, z=z_col))),
            pl.BlockSpec((tb, dv), idx(nblk, lambda h: (3 * nh * dk) // dv + h)),
        ]
    lb_specs = [pl.BlockSpec((n_layers, dk), lambda b, h, c_: (0, h)),
                pl.BlockSpec((n_layers, dk), lambda b, h, c_: (0, h)),
                pl.BlockSpec((n_layers, 1), lambda b, h, c_: (0, 0))]
    out_sd = jax.ShapeDtypeStruct((t, nh * dv), F32)
    return pl.pallas_call(
        functools.partial(_hgrn_kernel, chunk=c),
        grid=(batch, nh, nblk),
        in_specs=tok_specs + lb_specs + [_const_spec(a) for a in consts],
        out_specs=[pl.BlockSpec((tb, dv), _fwd_idx(nblk, lambda h: h)),
                   pl.BlockSpec((tb, dv), _bwd_idx(nblk, lambda h: h))],
        out_shape=[out_sd, out_sd],
        scratch_shapes=[pltpu.VMEM((dv, dk), F32), pltpu.VMEM((dv, dk), F32)],
        compiler_params=_params("parallel", "parallel", "arbitrary"),
        name="hgrn_scan",
    )(p_hg, p_hg, p_hg, p_hg, p_hg, p_hg, lb_fwd, lb_bwd, lmask, *consts)


def _headnorm_gate_kernel(of_ref, ob_ref, r_ref, gain_ref, o_ref, *, hd):
    o = of_ref[...] + ob_ref[...]
    gain = gain_ref[...]
    outs = []
    for h in range(o.shape[1] // hd):
        blk = o[:, h * hd:(h + 1) * hd]
        ms = jnp.mean(blk * blk, axis=-1, keepdims=True)
        outs.append(blk * lax.rsqrt(ms + NORM_EPS) * gain)
    y = jnp.concatenate(outs, axis=-1) * _silu(r_ref[...])
    o_ref[...] = y.astype(o_ref.dtype)


def _headnorm_gate(o_f, o_b, p, r_off, gain, hd, tm=512):
    t, w = o_f.shape
    return pl.pallas_call(
        functools.partial(_headnorm_gate_kernel, hd=hd),
        grid=(t // tm,),
        in_specs=[pl.BlockSpec((tm, w), lambda i: (i, 0)),
                  pl.BlockSpec((tm, w), lambda i: (i, 0)),
                  pl.BlockSpec((tm, w), lambda i: (i, r_off)),
                  pl.BlockSpec((1, hd), lambda i: (0, 0))],
        out_specs=pl.BlockSpec((tm, w), lambda i: (i, 0)),
        out_shape=jax.ShapeDtypeStruct((t, w), BF16),
        compiler_params=_params("parallel"),
        name="headnorm_gate",
    )(o_f, o_b, p, gain.reshape(1, hd))


def _rotary_tables(seq):
    half = ATTN_HEAD_DIM // 2
    inv_freq = ROPE_THETA ** (-jnp.arange(half, dtype=F32) / half)
    ang = jnp.arange(seq, dtype=jnp.int32).astype(F32)[:, None] * inv_freq[None, :]
    cos, sin = jnp.cos(ang), jnp.sin(ang)
    return jnp.concatenate([cos, cos], axis=-1), jnp.concatenate([-sin, sin], axis=-1)


def kernel(x, ffn1_norm, ffn1_w_in, ffn1_w_out, mix_norm, w_in, attn_q_norm, attn_k_norm, attn_lambda, attn_sub_norm, gla_gate_w2_fwd, gla_gate_b_fwd, gla_gate_w2_bwd, gla_gate_b_bwd, gla_out_norm, hgrn_lb_fwd, hgrn_lb_bwd, hgrn_out_norm, w_branch_attn, w_branch_gla, w_branch_hgrn, w_out, ffn2_norm, ffn2_w_in, ffn2_w_out):
    batch, seq, d = x.shape
    t = batch * seq
    depth = w_in.shape[0]
    cosf, sinf = _rotary_tables(seq)

    qk_w = 2 * ATTN_HEADS * 2 * ATTN_HEAD_DIM
    av_w = ATTN_HEADS * ATTN_V_DIM
    gla_w = 2 * GLA_HEADS * GLA_K_DIM + 2 * GLA_HEADS * GLA_V_DIM
    lr_w = 2 * GLA_GATE_RANK
    hg_w = 3 * HGRN_HEADS * HGRN_K_DIM + 2 * HGRN_HEADS * HGRN_V_DIM
    c_av, c_gla = qk_w, qk_w + av_w
    c_lr = c_gla + gla_w
    c_tail = c_lr + lr_w

    lmasks = np.tril(np.ones((depth, depth), np.float32))[:, :, None]

    tm, tn = 1024, 512
    tm_p, tn_p = 1024, 1024
    tm_w, tn_w = 512, 2048
    q_scale = ATTN_HEAD_DIM ** -0.5 * LOG2_E
    n_sub = qk_w // (2 * ATTN_HEAD_DIM)
    nrow = seq // tm_w
    rot_spec = pl.BlockSpec((tm_w, LANES), lambda j, i: (i % nrow, 0))

    w_in_t = jnp.swapaxes(w_in, 1, 2)
    w_qk_t = w_in_t[:, :qk_w, :].astype(BF16)

    xc = x.reshape(t, d)
    xg, rs = _rmsnorm(xc, ffn1_norm[0])
    for l in range(depth):
        lam_init = jnp.full((1, 1), 0.8 - 0.6 * math.exp(-0.3 * l), F32)
        xc, hn, hrs = _ffn(l, xc, xg, rs, ffn1_w_in, ffn1_w_out, mix_norm[l])

        qk_gain = jnp.concatenate([jnp.tile(attn_q_norm[l] * q_scale, n_sub),
                                   jnp.tile(attn_k_norm[l], n_sub)]).reshape(1, qk_w)
        qk = _matmul("proj_qk", l, [hn], [(w_qk_t, 0)], (0,), _ep_qk_norm_rotary,
                     qk_w, BF16, tm_w, tn_w, transposed=True,
                     extras=[(qk_gain, _row_spec(tn_w)), (cosf, rot_spec), (sinf, rot_spec),
                             (hrs, _rs_spec(tm_w))])
        proj = functools.partial(_matmul, layer=l, a_list=[hn], pairs=(0,), epilogue=_ep_scaled,
                                 tm=tm_p, tn=tn_p, transposed=True, stage_weights=True,
                                 extras=[(hrs, _rs_spec(tm_p))])
        p_av = proj("proj_av", w_list=[(w_in_t, c_av)], n_out=av_w, out_dtype=BF16)
        p_gla = proj("proj_gla", w_list=[(w_in_t, c_gla)], n_out=gla_w, out_dtype=F32)
        p_lr = proj("proj_lr", w_list=[(w_in_t, c_lr)], n_out=LANES, out_dtype=F32, tn=LANES)
        p_hg = proj("proj_hgrn", w_list=[(w_in_t, c_tail)], n_out=hg_w, out_dtype=F32)
        p_gate = proj("proj_gate", w_list=[(w_in_t, c_tail + hg_w)], n_out=3 * d, out_dtype=BF16)

        o_a = _attention(qk, p_av, attn_lambda[l], attn_sub_norm[l], lam_init, batch, seq)

        og_f, og_b = _gla_scan(p_gla, p_lr, gla_gate_w2_fwd[l], gla_gate_b_fwd[l],
                               gla_gate_w2_bwd[l], gla_gate_b_bwd[l], batch, seq)
        u_g = _headnorm_gate(og_f, og_b, p_gla, gla_w // (GLA_HEADS * GLA_V_DIM) - 1, gla_out_norm[l],
                             GLA_V_DIM)

        oh_f, oh_b = _hgrn_scan(p_hg, hgrn_lb_fwd, hgrn_lb_bwd, jnp.asarray(lmasks[l]), batch, seq)
        u_h = _headnorm_gate(oh_f, oh_b, p_hg, hg_w // (HGRN_HEADS * HGRN_V_DIM) - 1, hgrn_out_norm[l],
                             HGRN_V_DIM)

        merged = _matmul(
            "merge", l, [o_a, u_g, u_h],
            [(w_branch_attn, 0), (w_branch_gla, 0), (w_branch_hgrn, 0)], (0, 1, 2),
            _ep_gated_sum, d, BF16, tm, tn,
            extras=[(p_gate, _tile_spec(tm, tn, b * d // tn)) for b in range(3)])
        xc, xg, rs = _residual_matmul("out_proj", l, merged, w_out, xc, ffn2_norm[l], 1.0, tm, tn)

        xc, xg, rs = _ffn(l, xc, xg, rs, ffn2_w_in, ffn2_w_out, ffn1_norm[(l + 1) % depth])
    return xc.reshape(batch, seq, d)
```

```python
import functools
import math

import numpy as np
import jax
import jax.numpy as jnp
from jax import lax
from jax.experimental import pallas as pl
from jax.experimental.pallas import tpu as pltpu

F32 = jnp.float32
BF16 = jnp.bfloat16

NORM_EPS = 1e-6
LB_FLOOR = 1e-20
ROPE_THETA = 10000.0
GLA_GATE_NORMALIZER = 16.0

ATTN_HEADS = 8
ATTN_HEAD_DIM = 128
ATTN_V_DIM = 256
GLA_HEADS = 4
GLA_K_DIM = 128
GLA_V_DIM = 256
GLA_GATE_RANK = 16
HGRN_HEADS = 8
HGRN_K_DIM = 128
HGRN_V_DIM = 128

LANES = 128
VMEM_LIMIT_BYTES = 58 * 2**20
SCAN_CHUNK = 128
SCAN_BLOCK = 1024
ATTN_ROW_GROUPS = 4
LOG2_E = math.log2(math.e)


def _params(*sem):
    return pltpu.CompilerParams(dimension_semantics=sem, vmem_limit_bytes=VMEM_LIMIT_BYTES)


def _dot(a, b):
    return jnp.dot(a, b, preferred_element_type=F32)


def _dot_nt(a, b):
    return lax.dot_general(a, b, (((1,), (1,)), ((), ())), preferred_element_type=F32)


def _dot_tn(a, b):
    return lax.dot_general(a, b, (((0,), (0,)), ((), ())), preferred_element_type=F32)


def _sigmoid(x):
    return 1.0 / (1.0 + jnp.exp(-x))


def _silu(x):
    return x * _sigmoid(x)


def _log_sigmoid(x):
    return jnp.minimum(x, 0.0) - jnp.log1p(jnp.exp(-jnp.abs(x)))


def _split3(x):
    hi = x.astype(BF16)
    r1 = x - hi.astype(F32)
    mid = r1.astype(BF16)
    lo = (r1 - mid.astype(F32)).astype(BF16)
    return hi, mid, lo


def _rmsnorm_kernel(x_ref, g_ref, o_ref, rs_ref):
    x = x_ref[...]
    ms = jnp.mean(x * x, axis=-1, keepdims=True)
    o_ref[...] = (x * g_ref[...]).astype(o_ref.dtype)
    rs_ref[...] = jnp.broadcast_to(lax.rsqrt(ms + NORM_EPS), rs_ref.shape)


def _rmsnorm(x, gain, tm=256):
    t, d = x.shape
    return pl.pallas_call(
        _rmsnorm_kernel,
        grid=(t // tm,),
        in_specs=[pl.BlockSpec((tm, d), lambda i: (i, 0)),
                  pl.BlockSpec((1, d), lambda i: (0, 0))],
        out_specs=[pl.BlockSpec((tm, d), lambda i: (i, 0)),
                   pl.BlockSpec((tm, LANES), lambda i: (i, 0))],
        out_shape=[jax.ShapeDtypeStruct((t, d), BF16), jax.ShapeDtypeStruct((t, LANES), F32)],
        compiler_params=_params("parallel"),
        name="rmsnorm",
    )(x, gain.reshape(1, d))


def _weight_block_copy(w_hbm, stage, sem, j, *, layer, off, tn, transposed):
    if transposed:
        src = w_hbm.at[layer, pl.ds(pl.multiple_of(off + j * tn, 32), tn), :]
    else:
        src = w_hbm.at[layer, :, pl.ds(pl.multiple_of((j + off) * tn, LANES), tn)]
    return pltpu.make_async_copy(src, stage, sem)


def _mm_kernel(*refs, n_a, pairs, needs_cast, staged, n_extra, transposed, epilogue):
    n_w = len(pairs)
    a_refs = refs[:n_a]
    w_refs = refs[n_a:n_a + n_w]
    e_refs = refs[n_a + n_w:n_a + n_w + n_extra]
    o_ref = refs[n_a + n_w + n_extra]
    scratch = refs[n_a + n_w + n_extra + 1:]
    w_bf = []
    si = 0
    for k in range(n_w):
        if staged[k] is not None:
            sc, stage, sem = scratch[si:si + 3]
            si += 3
            copy = functools.partial(_weight_block_copy, w_refs[k], stage, sem, **staged[k])

            @pl.when(pl.program_id(1) == 0)
            def _(sc=sc, stage=stage, copy=copy):
                j = pl.program_id(0)

                @pl.when(j == 0)
                def _():
                    copy(j).start()

                copy(j).wait()
                sc[...] = stage[...].astype(BF16)

                @pl.when(j + 1 < pl.num_programs(0))
                def _():
                    copy(j + 1).start()

            w_bf.append(sc)
        elif needs_cast[k]:
            sc = scratch[si]
            si += 1

            @pl.when(pl.program_id(1) == 0)
            def _(sc=sc, w_ref=w_refs[k]):
                sc[...] = w_ref[...].reshape(sc.shape).astype(BF16)

            w_bf.append(sc)
        else:
            w_bf.append(w_refs[k])
    dot = _dot_nt if transposed else _dot
    accs = [dot(a_refs[pairs[k]][...], w_bf[k][...].reshape(w_bf[k].shape[-2:])) for k in range(n_w)]
    o_ref[...] = epilogue(accs, [e[...] for e in e_refs]).astype(o_ref.dtype)


def _matmul(name, layer, a_list, w_list, pairs, epilogue, n_out, out_dtype, tm, tn, extras=(),
            transposed=False, stage_weights=False):
    t = a_list[0].shape[0]
    in_specs, args, needs_cast, staged, scratch = [], [], [], [], []
    for a in a_list:
        in_specs.append(pl.BlockSpec((tm, a.shape[1]), lambda j, i: (i, 0)))
        args.append(a)
    for w, off in w_list:
        at = (layer,) if w.ndim == 3 else ()
        if stage_weights and w.ndim == 3 and w.dtype != BF16:
            blk = (tn, w.shape[-1]) if transposed else (w.shape[-2], tn)
            in_specs.append(pl.BlockSpec(memory_space=pl.ANY))
            scratch += [pltpu.VMEM(blk, BF16), pltpu.VMEM(blk, w.dtype), pltpu.SemaphoreType.DMA(())]
            staged.append(dict(layer=layer, off=off, tn=tn, transposed=transposed))
            needs_cast.append(True)
            args.append(w)
            continue
        staged.append(None)
        if transposed:
            blk = (tn, w.shape[-1])
            spec = pl.BlockSpec(tuple(pl.Element(n) for n in (1,) * len(at) + blk),
                                functools.partial(lambda j, i, o, at: at + (pl.multiple_of(o + j * tn, 32), 0),
                                                  o=off, at=at))
        else:
            blk = (w.shape[-2], tn)
            spec = pl.BlockSpec((None,) * len(at) + blk,
                                functools.partial(lambda j, i, o, at: at + (0, j + o), o=off, at=at))
        in_specs.append(spec)
        if w.dtype != BF16:
            scratch.append(pltpu.VMEM(blk, BF16))
        needs_cast.append(w.dtype != BF16)
        args.append(w)
    for e, spec in extras:
        in_specs.append(spec)
        args.append(e)
    return pl.pallas_call(
        functools.partial(_mm_kernel, n_a=len(a_list), pairs=tuple(pairs), needs_cast=tuple(needs_cast),
                          staged=tuple(staged), n_extra=len(extras), transposed=transposed, epilogue=epilogue),
        grid=(n_out // tn, t // tm),
        in_specs=in_specs,
        out_specs=pl.BlockSpec((tm, tn), lambda j, i: (i, j)),
        out_shape=jax.ShapeDtypeStruct((t, n_out), out_dtype),
        scratch_shapes=scratch,
        compiler_params=_params("arbitrary", "arbitrary"),
        name=name,
    )(*args)


def _tile_spec(tm, tn, off=0):
    return pl.BlockSpec((tm, tn), functools.partial(lambda j, i, o: (i, j + o), o=off))


def _row_spec(tn):
    return pl.BlockSpec((1, tn), lambda j, i: (0, j))


def _rs_spec(tm):
    return pl.BlockSpec((tm, LANES), lambda j, i: (i, 0))


def _lane_tile(rs, n):
    return rs if n == LANES else jnp.concatenate([rs] * (n // LANES), axis=1)


def _ep_scaled(accs, extras):
    return accs[0] * _lane_tile(extras[0], accs[0].shape[1])


def _ep_swiglu(accs, extras):
    rs = _lane_tile(extras[0], accs[0].shape[1])
    return _silu(accs[0] * rs) * (accs[1] * rs)


def _ep_gated_sum(accs, extras):
    gates = [_sigmoid(e.astype(F32)) for e in extras]
    return gates[0] * accs[0] + gates[1] * accs[1] + gates[2] * accs[2]


def _ep_qk_norm_rotary(accs, extras):
    acc = accs[0]
    gain, cosf, sinf, rs = extras
    outs = []
    for c in range(acc.shape[1] // LANES):
        blk = acc[:, c * LANES:(c + 1) * LANES] * rs
        ms = jnp.mean(blk * blk, axis=-1, keepdims=True)
        y = blk * lax.rsqrt(ms + NORM_EPS) * gain[:, c * LANES:(c + 1) * LANES]
        outs.append(y * cosf + pltpu.roll(y, LANES // 2, 1) * sinf)
    return jnp.concatenate(outs, axis=-1)


def _residual_kernel(a_ref, w_ref, x_ref, g_ref, o_ref, xg_ref, rs_ref, wbf_ref, ss_ref, *, scale, d_model):
    j, i = pl.program_id(0), pl.program_id(1)
    tm, tn = o_ref.shape

    @pl.when(i == 0)
    def _():
        wbf_ref[...] = w_ref[...].astype(BF16)

    xn = x_ref[...] + scale * _dot(a_ref[...], wbf_ref[...])
    o_ref[...] = xn
    xg_ref[...] = (xn * g_ref[...]).astype(xg_ref.dtype)
    sq = xn * xn
    part = sq[:, 0:LANES]
    for c in range(1, tn // LANES):
        part = part + sq[:, c * LANES:(c + 1) * LANES]
    part = jnp.broadcast_to(jnp.sum(part, axis=-1, keepdims=True), (tm, LANES))
    rows = pl.ds(pl.multiple_of(i * tm, tm), tm)

    @pl.when(j == 0)
    def _():
        ss_ref[rows, :] = part

    @pl.when(j != 0)
    def _():
        ss_ref[rows, :] = ss_ref[rows, :] + part

    rs_ref[...] = lax.rsqrt(ss_ref[rows, :] * (1.0 / d_model) + NORM_EPS)


def _residual_matmul(name, layer, a, w, x, gain_next, scale, tm, tn):
    t, d = x.shape
    k_dim = w.shape[1]
    x_new, xg, rs_partial = pl.pallas_call(
        functools.partial(_residual_kernel, scale=scale, d_model=d),
        grid=(d // tn, t // tm),
        in_specs=[pl.BlockSpec((tm, k_dim), lambda j, i: (i, 0)),
                  pl.BlockSpec((None, k_dim, tn), lambda j, i: (layer, 0, j)),
                  pl.BlockSpec((tm, tn), lambda j, i: (i, j)),
                  pl.BlockSpec((1, tn), lambda j, i: (0, j))],
        out_specs=[pl.BlockSpec((tm, tn), lambda j, i: (i, j)),
                   pl.BlockSpec((tm, tn), lambda j, i: (i, j)),
                   pl.BlockSpec((None, tm, LANES), lambda j, i: (j, i, 0))],
        out_shape=[jax.ShapeDtypeStruct((t, d), F32), jax.ShapeDtypeStruct((t, d), BF16),
                   jax.ShapeDtypeStruct((d // tn, t, LANES), F32)],
        scratch_shapes=[pltpu.VMEM((k_dim, tn), BF16), pltpu.VMEM((t, LANES), F32)],
        compiler_params=_params("arbitrary", "arbitrary"),
        name=name,
    )(a, w, x, gain_next.reshape(1, d))
    return x_new, xg, rs_partial[d // tn - 1]


def _ffn(layer, x, xg, rs, w_in, w_out, gain_next):
    f = w_out.shape[1]
    tm, tf = 1024, 512
    h = _matmul("ffn_in", layer, [xg], [(w_in, 0), (w_in, f // tf)], (0, 0), _ep_swiglu,
                f, BF16, tm, tf, extras=[(rs, _rs_spec(tm))], stage_weights=True)
    return _residual_matmul("ffn_out", layer, h, w_out, x, gain_next, 0.5, 512, 1024)


def _attn_kernel(q_ref, k_ref, v_ref, lam_ref, gain_ref, linit_ref, o_ref):
    dh = ATTN_HEAD_DIM
    v = v_ref[...]
    lv = lam_ref[...]
    lam_init = linit_ref[...]
    s01 = jnp.sum(lv[0:1] * lv[1:2], axis=-1, keepdims=True)
    s23 = jnp.sum(lv[2:3] * lv[3:4], axis=-1, keepdims=True)
    lam = jnp.exp(s01) - jnp.exp(s23) + lam_init

    def scores(rows):
        return [_dot_nt(q_ref[rows, m * dh:(m + 1) * dh], k_ref[:, m * dh:(m + 1) * dh]) for m in range(2)]

    def softmax(s):
        p = [jnp.exp2(s[m] - jnp.max(s[m], axis=-1, keepdims=True)) for m in range(2)]
        return p, [1.0 / jnp.sum(p[m], axis=-1, keepdims=True) for m in range(2)]

    def values(p, r):
        o1, o2 = [_dot(p[m].astype(BF16), v) for m in range(2)]
        return o1 * r[0] - o2 * (lam * r[1])

    rows = q_ref.shape[0] // ATTN_ROW_GROUPS
    groups = [slice(g * rows, (g + 1) * rows) for g in range(ATTN_ROW_GROUPS)]
    s_next = scores(groups[0])
    outs = []
    for g in range(ATTN_ROW_GROUPS):
        s_cur = s_next
        if g + 1 < ATTN_ROW_GROUPS:
            s_next = scores(groups[g + 1])
        outs.append(values(*softmax(s_cur)))
    o = jnp.concatenate(outs, axis=0)
    ms = jnp.mean(o * o, axis=-1, keepdims=True)
    o = o * lax.rsqrt(ms + NORM_EPS) * gain_ref[...] * (1.0 - lam_init)
    o_ref[...] = o.astype(o_ref.dtype)


def _attention(qk, v, lam_vec, sub_gain, lam_init, batch, seq, tq=2048):
    t = qk.shape[0]
    hd = 2 * ATTN_HEAD_DIM
    nq = seq // tq
    return pl.pallas_call(
        _attn_kernel,
        grid=(batch, ATTN_HEADS, nq),
        in_specs=[
            pl.BlockSpec((tq, hd), lambda b, h, i: (b * nq + i, h)),
            pl.BlockSpec((seq, hd), lambda b, h, i: (b, ATTN_HEADS + h)),
            pl.BlockSpec((seq, ATTN_V_DIM), lambda b, h, i: (b, h)),
            pl.BlockSpec((4, ATTN_HEAD_DIM), lambda b, h, i: (0, 0)),
            pl.BlockSpec((1, ATTN_V_DIM), lambda b, h, i: (0, 0)),
            pl.BlockSpec((1, 1), lambda b, h, i: (0, 0)),
        ],
        out_specs=pl.BlockSpec((tq, ATTN_V_DIM), lambda b, h, i: (b * nq + i, h)),
        out_shape=jax.ShapeDtypeStruct((t, ATTN_HEADS * ATTN_V_DIM), BF16),
        compiler_params=_params("parallel", "parallel", "arbitrary"),
        name="diff_attention",
    )(qk, qk, v, lam_vec, sub_gain.reshape(1, -1), lam_init.reshape(1, 1))


def _scan_constants(c, rev):
    idx = np.arange(c)
    if rev:
        cum = idx[None, :] >= idx[:, None]
    else:
        cum = idx[None, :] <= idx[:, None]
    valids = [np.eye(c, dtype=np.float32)]
    s = c // 2
    while s >= 1:
        blk = idx // (2 * s)
        late = (idx // s) % 2
        same = blk[:, None] == blk[None, :]
        if not rev:
            valid = same & (late[:, None] == 1) & (late[None, :] == 0)
        else:
            valid = same & (late[:, None] == 0) & (late[None, :] == 1)
        valids.append(valid.astype(np.float32))
        s //= 2
    return (jnp.asarray(cum.astype(np.float32), dtype=BF16),
            jnp.asarray(np.stack(valids), dtype=F32))


def _level_masks(c, dk, n_lvl, rev):
    row = lax.broadcasted_iota(jnp.int32, (c, dk), 0)
    masks = []
    for lvl in range(1, n_lvl + 1):
        late = ((row >> (n_lvl - lvl)) & 1) == 1
        masks.append(jnp.logical_not(late) if rev else late)
    return masks


def _mid_row(cum, s):
    c, dk = cum.shape
    if s >= 8:
        parts = [jnp.broadcast_to(cum[b * 2 * s + s:b * 2 * s + s + 1, :], (2 * s, dk))
                 for b in range(c // (2 * s))]
        return parts[0] if len(parts) == 1 else jnp.concatenate(parts, axis=0)
    x3 = cum.reshape(c // 8, 8, dk)
    sub = lax.broadcasted_iota(jnp.int32, x3.shape, 1)

    def row(r):
        return jnp.broadcast_to(x3[:, r:r + 1, :], x3.shape)

    if s == 4:
        r3 = row(4)
    elif s == 2:
        r3 = jnp.where(sub < 4, row(2), row(6))
    else:
        r3 = jnp.where(sub < 2, row(1), jnp.where(sub < 4, row(3), jnp.where(sub < 6, row(5), row(7))))
    return r3.reshape(c, dk)


def _scan_chunk(q, k, v, g, st, cum_ref, valid_ref, masks, rev):
    c = q.shape[0]
    n_lvl = len(masks)
    tri = cum_ref[...]
    gh, gm, gl = _split3(g)
    cum = _dot(tri, gh) + _dot(tri, gm) + _dot(tri, gl)
    total = cum[0:1] if rev else cum[c - 1:c]
    a = valid_ref[0] * _dot_nt(q.astype(BF16), k.astype(BF16))
    for lvl in range(1, n_lvl + 1):
        e = -jnp.abs(cum - _mid_row(cum, c >> lvl))
        x = (jnp.where(masks[lvl - 1], q, k) * jnp.exp2(e)).astype(BF16)
        a = a + valid_ref[lvl] * _dot_nt(x, x)
    qd = (q * jnp.exp2(cum)).astype(BF16)
    kd = (k * jnp.exp2(total - cum)).astype(BF16)
    vb = v.astype(BF16)
    o = _dot(a.astype(BF16), vb) + _dot_nt(qd, st.astype(BF16))
    return o, st * jnp.exp2(total) + _dot_tn(vb, kd)


def _gla_gate(lr, w2, bias):
    lr_hi = lr.astype(BF16)
    lr_lo = (lr - lr_hi.astype(F32)).astype(BF16)
    w2_hi = w2.astype(BF16)
    w2_lo = (w2 - w2_hi.astype(F32)).astype(BF16)
    z = _dot(lr_hi, w2_hi) + _dot(lr_hi, w2_lo) + _dot(lr_lo, w2_hi) + bias
    return _log_sigmoid(z) * (LOG2_E / GLA_GATE_NORMALIZER)


def _gla_kernel(qf_ref, kf_ref, vf_ref, lrf_ref, qb_ref, kb_ref, vb_ref, lrb_ref,
                w2f_ref, bf_ref, w2b_ref, bb_ref, dallf_ref, validf_ref, dallb_ref, validb_ref,
                of_ref, ob_ref, stf_ref, stb_ref, *, chunk):
    @pl.when(pl.program_id(2) == 0)
    def _():
        stf_ref[...] = jnp.zeros_like(stf_ref)
        stb_ref[...] = jnp.zeros_like(stb_ref)

    r = GLA_GATE_RANK
    gf_all = _gla_gate(lrf_ref[...][:, 0:r], w2f_ref[...], bf_ref[...])
    gb_all = _gla_gate(lrb_ref[...][:, r:2 * r], w2b_ref[...], bb_ref[...])
    n_lvl = validf_ref.shape[0] - 1
    mf = _level_masks(chunk, GLA_K_DIM, n_lvl, False)
    mb = _level_masks(chunk, GLA_K_DIM, n_lvl, True)
    scale = GLA_K_DIM ** -0.5
    n = qf_ref.shape[0] // chunk
    stf, stb = stf_ref[...], stb_ref[...]
    for ci in range(n):
        sf = slice(ci * chunk, (ci + 1) * chunk)
        sb = slice((n - 1 - ci) * chunk, (n - ci) * chunk)
        o, stf = _scan_chunk(qf_ref[sf, :] * scale, kf_ref[sf, :], vf_ref[sf, :], gf_all[sf, :],
                             stf, dallf_ref, validf_ref, mf, False)
        of_ref[sf, :] = o
        o, stb = _scan_chunk(qb_ref[sb, :] * scale, kb_ref[sb, :], vb_ref[sb, :], gb_all[sb, :],
                             stb, dallb_ref, validb_ref, mb, True)
        ob_ref[sb, :] = o
    stf_ref[...] = stf
    stb_ref[...] = stb


def _hgrn_lower_bound(logits, lmask):
    pe = jnp.exp(logits - jnp.max(logits, axis=0, keepdims=True))
    p = pe / jnp.sum(pe, axis=0, keepdims=True)
    lb = jnp.sum(p * lmask, axis=0, keepdims=True) - p[0:1]
    lb = jnp.clip(lb, 0.0, 1.0 - 1e-6)
    return lb, jnp.log(jnp.maximum(lb, LB_FLOOR)), jnp.log1p(-lb)


def _hgrn_gate(z, lb, log_lb, log_1m):
    t2 = log_1m + _log_sigmoid(z)
    g = jnp.maximum(log_lb, t2) + jnp.log1p(jnp.exp(-jnp.abs(log_lb - t2)))
    return (1.0 - lb) * _sigmoid(-z), g * LOG2_E


def _hgrn_kernel(qf_ref, zf_ref, vf_ref, qb_ref, zb_ref, vb_ref, lbf_ref, lbb_ref, lmask_ref,
                 dallf_ref, validf_ref, dallb_ref, validb_ref, of_ref, ob_ref, stf_ref, stb_ref,
                 *, chunk):
    @pl.when(pl.program_id(2) == 0)
    def _():
        stf_ref[...] = jnp.zeros_like(stf_ref)
        stb_ref[...] = jnp.zeros_like(stb_ref)

    lbf = _hgrn_lower_bound(lbf_ref[...], lmask_ref[...])
    lbb = _hgrn_lower_bound(lbb_ref[...], lmask_ref[...])
    n_lvl = validf_ref.shape[0] - 1
    mf = _level_masks(chunk, HGRN_K_DIM, n_lvl, False)
    mb = _level_masks(chunk, HGRN_K_DIM, n_lvl, True)
    scale = HGRN_K_DIM ** -0.5
    n = qf_ref.shape[0] // chunk
    stf, stb = stf_ref[...], stb_ref[...]
    for ci in range(n):
        sf = slice(ci * chunk, (ci + 1) * chunk)
        sb = slice((n - 1 - ci) * chunk, (n - ci) * chunk)
        k, g = _hgrn_gate(zf_ref[sf, :], *lbf)
        o, stf = _scan_chunk(qf_ref[sf, :] * scale, k, vf_ref[sf, :], g,
                             stf, dallf_ref, validf_ref, mf, False)
        of_ref[sf, :] = o
        k, g = _hgrn_gate(zb_ref[sb, :], *lbb)
        o, stb = _scan_chunk(qb_ref[sb, :] * scale, k, vb_ref[sb, :], g,
                             stb, dallb_ref, validb_ref, mb, True)
        ob_ref[sb, :] = o
    stf_ref[...] = stf
    stb_ref[...] = stb


def _fwd_idx(nblk, col):
    return lambda b, h, c: (b * nblk + c, col(h))


def _bwd_idx(nblk, col):
    return lambda b, h, c: (b * nblk + (nblk - 1 - c), col(h))


def _const_spec(arr):
    zeros = (0,) * arr.ndim
    return pl.BlockSpec(arr.shape, lambda b, h, c: zeros)


def _gla_scan(p_gla, lr, w2f, bf, w2b, bb, batch, seq):
    t = p_gla.shape[0]
    tb, c = SCAN_BLOCK, SCAN_CHUNK
    nblk = seq // tb
    dk, dv, nh = GLA_K_DIM, GLA_V_DIM, GLA_HEADS
    consts = _scan_constants(c, False) + _scan_constants(c, True)
    tok_specs = []
    for idx in (_fwd_idx, _bwd_idx):
        tok_specs += [
            pl.BlockSpec((tb, dk), idx(nblk, lambda h: h)),
            pl.BlockSpec((tb, dk), idx(nblk, lambda h: nh + h)),
            pl.BlockSpec((tb, dv), idx(nblk, lambda h: (2 * nh * dk) // dv + h)),
            pl.BlockSpec((tb, lr.shape[1]), idx(nblk, lambda h: 0)),
        ]
    gate_specs = [pl.BlockSpec((GLA_GATE_RANK, dk), lambda b, h, c_: (0, h)),
                  pl.BlockSpec((1, dk), lambda b, h, c_: (0, h))] * 2
    out_sd = jax.ShapeDtypeStruct((t, nh * dv), F32)
    return pl.pallas_call(
        functools.partial(_gla_kernel, chunk=c),
        grid=(batch, nh, nblk),
        in_specs=tok_specs + gate_specs + [_const_spec(a) for a in consts],
        out_specs=[pl.BlockSpec((tb, dv), _fwd_idx(nblk, lambda h: h)),
                   pl.BlockSpec((tb, dv), _bwd_idx(nblk, lambda h: h))],
        out_shape=[out_sd, out_sd],
        scratch_shapes=[pltpu.VMEM((dv, dk), F32), pltpu.VMEM((dv, dk), F32)],
        compiler_params=_params("parallel", "parallel", "arbitrary"),
        name="gla_scan",
    )(p_gla, p_gla, p_gla, lr, p_gla, p_gla, p_gla, lr,
      w2f, bf.reshape(1, -1), w2b, bb.reshape(1, -1), *consts)


def _hgrn_scan(p_hg, lb_fwd, lb_bwd, lmask, batch, seq):
    t = p_hg.shape[0]
    tb, c = SCAN_BLOCK, SCAN_CHUNK
    nblk = seq // tb
    dk, dv, nh = HGRN_K_DIM, HGRN_V_DIM, HGRN_HEADS
    consts = _scan_constants(c, False) + _scan_constants(c, True)
    n_layers = lb_fwd.shape[0]
    tok_specs = []
    for idx, z_col in ((_fwd_idx, nh), (_bwd_idx, 2 * nh)):
        tok_specs += [
            pl.BlockSpec((tb, dk), idx(nblk, lambda h: h)),
            pl.BlockSpec((tb, dk), idx(nblk, functools.partial(lambda h, z: z + h, z=z_col))),
            pl.BlockSpec((tb, dv), idx(nblk, lambda h: (3 * nh * dk) // dv + h)),
        ]
    lb_specs = [pl.BlockSpec((n_layers, dk), lambda b, h, c_: (0, h)),
                pl.BlockSpec((n_layers, dk), lambda b, h, c_: (0, h)),
                pl.BlockSpec((n_layers, 1), lambda b, h, c_: (0, 0))]
    out_sd = jax.ShapeDtypeStruct((t, nh * dv), F32)
    return pl.pallas_call(
        functools.partial(_hgrn_kernel, chunk=c),
        grid=(batch, nh, nblk),
        in_specs=tok_specs + lb_specs + [_const_spec(a) for a in consts],
        out_specs=[pl.BlockSpec((tb, dv), _fwd_idx(nblk, lambda h: h)),
                   pl.BlockSpec((tb, dv), _bwd_idx(nblk, lambda h: h))],
        out_shape=[out_sd, out_sd],
        scratch_shapes=[pltpu.VMEM((dv, dk), F32), pltpu.VMEM((dv, dk), F32)],
        compiler_params=_params("parallel", "parallel", "arbitrary"),
        name="hgrn_scan",
    )(p_hg, p_hg, p_hg, p_hg, p_hg, p_hg, lb_fwd, lb_bwd, lmask, *consts)


def _headnorm_gate_kernel(of_ref, ob_ref, r_ref, gain_ref, o_ref, *, hd):
    o = of_ref[...] + ob_ref[...]
    gain = gain_ref[...]
    outs = []
    for h in range(o.shape[1] // hd):
        blk = o[:, h * hd:(h + 1) * hd]
        ms = jnp.mean(blk * blk, axis=-1, keepdims=True)
        outs.append(blk * lax.rsqrt(ms + NORM_EPS) * gain)
    y = jnp.concatenate(outs, axis=-1) * _silu(r_ref[...])
    o_ref[...] = y.astype(o_ref.dtype)


def _headnorm_gate(o_f, o_b, p, r_off, gain, hd, tm=512):
    t, w = o_f.shape
    return pl.pallas_call(
        functools.partial(_headnorm_gate_kernel, hd=hd),
        grid=(t // tm,),
        in_specs=[pl.BlockSpec((tm, w), lambda i: (i, 0)),
                  pl.BlockSpec((tm, w), lambda i: (i, 0)),
                  pl.BlockSpec((tm, w), lambda i: (i, r_off)),
                  pl.BlockSpec((1, hd), lambda i: (0, 0))],
        out_specs=pl.BlockSpec((tm, w), lambda i: (i, 0)),
        out_shape=jax.ShapeDtypeStruct((t, w), BF16),
        compiler_params=_params("parallel"),
        name="headnorm_gate",
    )(o_f, o_b, p, gain.reshape(1, hd))


def _rotary_tables(seq):
    half = ATTN_HEAD_DIM // 2
    inv_freq = ROPE_THETA ** (-jnp.arange(half, dtype=F32) / half)
    ang = jnp.arange(seq, dtype=jnp.int32).astype(F32)[:, None] * inv_freq[None, :]
    cos, sin = jnp.cos(ang), jnp.sin(ang)
    return jnp.concatenate([cos, cos], axis=-1), jnp.concatenate([-sin, sin], axis=-1)


def kernel(x, ffn1_norm, ffn1_w_in, ffn1_w_out, mix_norm, w_in, attn_q_norm, attn_k_norm, attn_lambda, attn_sub_norm, gla_gate_w2_fwd, gla_gate_b_fwd, gla_gate_w2_bwd, gla_gate_b_bwd, gla_out_norm, hgrn_lb_fwd, hgrn_lb_bwd, hgrn_out_norm, w_branch_attn, w_branch_gla, w_branch_hgrn, w_out, ffn2_norm, ffn2_w_in, ffn2_w_out):
    batch, seq, d = x.shape
    t = batch * seq
    depth = w_in.shape[0]
    cosf, sinf = _rotary_tables(seq)

    qk_w = 2 * ATTN_HEADS * 2 * ATTN_HEAD_DIM
    av_w = ATTN_HEADS * ATTN_V_DIM
    gla_w = 2 * GLA_HEADS * GLA_K_DIM + 2 * GLA_HEADS * GLA_V_DIM
    lr_w = 2 * GLA_GATE_RANK
    hg_w = 3 * HGRN_HEADS * HGRN_K_DIM + 2 * HGRN_HEADS * HGRN_V_DIM
    c_av, c_gla = qk_w, qk_w + av_w
    c_lr = c_gla + gla_w
    c_tail = c_lr + lr_w

    lmasks = np.tril(np.ones((depth, depth), np.float32))[:, :, None]

    tm, tn = 1024, 512
    tm_p, tn_p = 1024, 1024
    tm_w, tn_w = 512, 2048
    q_scale = ATTN_HEAD_DIM ** -0.5 * LOG2_E
    n_sub = qk_w // (2 * ATTN_HEAD_DIM)
    nrow = seq // tm_w
    rot_spec = pl.BlockSpec((tm_w, LANES), lambda j, i: (i % nrow, 0))

    w_in_t = jnp.swapaxes(w_in, 1, 2)
    w_qk_t = w_in_t[:, :qk_w, :].astype(BF16)

    xc = x.reshape(t, d)
    xg, rs = _rmsnorm(xc, ffn1_norm[0])
    for l in range(depth):
        lam_init = jnp.full((1, 1), 0.8 - 0.6 * math.exp(-0.3 * l), F32)
        xc, hn, hrs = _ffn(l, xc, xg, rs, ffn1_w_in, ffn1_w_out, mix_norm[l])

        qk_gain = jnp.concatenate([jnp.tile(attn_q_norm[l] * q_scale, n_sub),
                                   jnp.tile(attn_k_norm[l], n_sub)]).reshape(1, qk_w)
        qk = _matmul("proj_qk", l, [hn], [(w_qk_t, 0)], (0,), _ep_qk_norm_rotary,
                     qk_w, BF16, tm_w, tn_w, transposed=True,
                     extras=[(qk_gain, _row_spec(tn_w)), (cosf, rot_spec), (sinf, rot_spec),
                             (hrs, _rs_spec(tm_w))])
        proj = functools.partial(_matmul, layer=l, a_list=[hn], pairs=(0,), epilogue=_ep_scaled,
                                 tm=tm_p, tn=tn_p, transposed=True, stage_weights=True,
                                 extras=[(hrs, _rs_spec(tm_p))])
        p_av = proj("proj_av", w_list=[(w_in_t, c_av)], n_out=av_w, out_dtype=BF16)
        p_gla = proj("proj_gla", w_list=[(w_in_t, c_gla)], n_out=gla_w, out_dtype=F32)
        p_lr = proj("proj_lr", w_list=[(w_in_t, c_lr)], n_out=LANES, out_dtype=F32, tn=LANES)
        p_hg = proj("proj_hgrn", w_list=[(w_in_t, c_tail)], n_out=hg_w, out_dtype=F32)
        p_gate = proj("proj_gate", w_list=[(w_in_t, c_tail + hg_w)], n_out=3 * d, out_dtype=BF16)

        o_a = _attention(qk, p_av, attn_lambda[l], attn_sub_norm[l], lam_init, batch, seq)

        og_f, og_b = _gla_scan(p_gla, p_lr, gla_gate_w2_fwd[l], gla_gate_b_fwd[l],
                               gla_gate_w2_bwd[l], gla_gate_b_bwd[l], batch, seq)
        u_g = _headnorm_gate(og_f, og_b, p_gla, gla_w // (GLA_HEADS * GLA_V_DIM) - 1, gla_out_norm[l],
                             GLA_V_DIM)

        oh_f, oh_b = _hgrn_scan(p_hg, hgrn_lb_fwd, hgrn_lb_bwd, jnp.asarray(lmasks[l]), batch, seq)
        u_h = _headnorm_gate(oh_f, oh_b, p_hg, hg_w // (HGRN_HEADS * HGRN_V_DIM) - 1, hgrn_out_norm[l],
                             HGRN_V_DIM)

        merged = _matmul(
            "merge", l, [o_a, u_g, u_h],
            [(w_branch_attn, 0), (w_branch_gla, 0), (w_branch_hgrn, 0)], (0, 1, 2),
            _ep_gated_sum, d, BF16, tm, tn,
            extras=[(p_gate, _tile_spec(tm, tn, b * d // tn)) for b in range(3)])
        xc, xg, rs = _residual_matmul("out_proj", l, merged, w_out, xc, ffn2_norm[l], 1.0, tm, tn)

        xc, xg, rs = _ffn(l, xc, xg, rs, ffn2_w_in, ffn2_w_out, ffn1_norm[(l + 1) % depth])
    return xc.reshape(batch, seq, d)
```

```python
import functools
import math

import numpy as np
import jax
import jax.numpy as jnp
from jax import lax
from jax.experimental import pallas as pl
from jax.experimental.pallas import tpu as pltpu

F32 = jnp.float32
BF16 = jnp.bfloat16

NORM_EPS = 1e-6
LB_FLOOR = 1e-20
ROPE_THETA = 10000.0
GLA_GATE_NORMALIZER = 16.0

ATTN_HEADS = 8
ATTN_HEAD_DIM = 128
ATTN_V_DIM = 256
GLA_HEADS = 4
GLA_K_DIM = 128
GLA_V_DIM = 256
GLA_GATE_RANK = 16
HGRN_HEADS = 8
HGRN_K_DIM = 128
HGRN_V_DIM = 128

LANES = 128
VMEM_LIMIT_BYTES = 58 * 2**20
SCAN_CHUNK = 128
SCAN_BLOCK = 1024
ATTN_ROW_GROUPS = 4
LOG2_E = math.log2(math.e)


def _params(*sem):
    return pltpu.CompilerParams(dimension_semantics=sem, vmem_limit_bytes=VMEM_LIMIT_BYTES)


def _dot(a, b):
    return jnp.dot(a, b, preferred_element_type=F32)


def _dot_nt(a, b):
    return lax.dot_general(a, b, (((1,), (1,)), ((), ())), preferred_element_type=F32)


def _dot_tn(a, b):
    return lax.dot_general(a, b, (((0,), (0,)), ((), ())), preferred_element_type=F32)


def _sigmoid(x):
    return 1.0 / (1.0 + jnp.exp(-x))


def _silu(x):
    return x * _sigmoid(x)


def _log_sigmoid(x):
    return jnp.minimum(x, 0.0) - jnp.log1p(jnp.exp(-jnp.abs(x)))


def _split3(x):
    hi = x.astype(BF16)
    r1 = x - hi.astype(F32)
    mid = r1.astype(BF16)
    lo = (r1 - mid.astype(F32)).astype(BF16)
    return hi, mid, lo


def _rmsnorm_kernel(x_ref, g_ref, o_ref, rs_ref):
    x = x_ref[...]
    ms = jnp.mean(x * x, axis=-1, keepdims=True)
    o_ref[...] = (x * g_ref[...]).astype(o_ref.dtype)
    rs_ref[...] = jnp.broadcast_to(lax.rsqrt(ms + NORM_EPS), rs_ref.shape)


def _rmsnorm(x, gain, tm=256):
    t, d = x.shape
    return pl.pallas_call(
        _rmsnorm_kernel,
        grid=(t // tm,),
        in_specs=[pl.BlockSpec((tm, d), lambda i: (i, 0)),
                  pl.BlockSpec((1, d), lambda i: (0, 0))],
        out_specs=[pl.BlockSpec((tm, d), lambda i: (i, 0)),
                   pl.BlockSpec((tm, LANES), lambda i: (i, 0))],
        out_shape=[jax.ShapeDtypeStruct((t, d), BF16), jax.ShapeDtypeStruct((t, LANES), F32)],
        compiler_params=_params("parallel"),
        name="rmsnorm",
    )(x, gain.reshape(1, d))


def _weight_block_copy(w_hbm, stage, sem, j, *, layer, off, tn, transposed):
    if transposed:
        src = w_hbm.at[layer, pl.ds(pl.multiple_of(off + j * tn, 32), tn), :]
    else:
        src = w_hbm.at[layer, :, pl.ds(pl.multiple_of((j + off) * tn, LANES), tn)]
    return pltpu.make_async_copy(src, stage, sem)


def _mm_kernel(*refs, n_a, pairs, needs_cast, staged, n_extra, transposed, epilogue):
    n_w = len(pairs)
    a_refs = refs[:n_a]
    w_refs = refs[n_a:n_a + n_w]
    e_refs = refs[n_a + n_w:n_a + n_w + n_extra]
    o_ref = refs[n_a + n_w + n_extra]
    scratch = refs[n_a + n_w + n_extra + 1:]
    w_bf = []
    si = 0
    for k in range(n_w):
        if staged[k] is not None:
            sc, stage, sem = scratch[si:si + 3]
            si += 3
            copy = functools.partial(_weight_block_copy, w_refs[k], stage, sem, **staged[k])

            @pl.when(pl.program_id(1) == 0)
            def _(sc=sc, stage=stage, copy=copy):
                j = pl.program_id(0)

                @pl.when(j == 0)
                def _():
                    copy(j).start()

                copy(j).wait()
                sc[...] = stage[...].astype(BF16)

                @pl.when(j + 1 < pl.num_programs(0))
                def _():
                    copy(j + 1).start()

            w_bf.append(sc)
        elif needs_cast[k]:
            sc = scratch[si]
            si += 1

            @pl.when(pl.program_id(1) == 0)
            def _(sc=sc, w_ref=w_refs[k]):
                sc[...] = w_ref[...].reshape(sc.shape).astype(BF16)

            w_bf.append(sc)
        else:
            w_bf.append(w_refs[k])
    dot = _dot_nt if transposed else _dot
    accs = [dot(a_refs[pairs[k]][...], w_bf[k][...].reshape(w_bf[k].shape[-2:])) for k in range(n_w)]
    o_ref[...] = epilogue(accs, [e[...] for e in e_refs]).astype(o_ref.dtype)


def _matmul(name, layer, a_list, w_list, pairs, epilogue, n_out, out_dtype, tm, tn, extras=(),
            transposed=False, stage_weights=False):
    t = a_list[0].shape[0]
    in_specs, args, needs_cast, staged, scratch = [], [], [], [], []
    for a in a_list:
        in_specs.append(pl.BlockSpec((tm, a.shape[1]), lambda j, i: (i, 0)))
        args.append(a)
    for w, off in w_list:
        at = (layer,) if w.ndim == 3 else ()
        if stage_weights and w.ndim == 3 and w.dtype != BF16:
            blk = (tn, w.shape[-1]) if transposed else (w.shape[-2], tn)
            in_specs.append(pl.BlockSpec(memory_space=pl.ANY))
            scratch += [pltpu.VMEM(blk, BF16), pltpu.VMEM(blk, w.dtype), pltpu.SemaphoreType.DMA(())]
            staged.append(dict(layer=layer, off=off, tn=tn, transposed=transposed))
            needs_cast.append(True)
            args.append(w)
            continue
        staged.append(None)
        if transposed:
            blk = (tn, w.shape[-1])
            spec = pl.BlockSpec(tuple(pl.Element(n) for n in (1,) * len(at) + blk),
                                functools.partial(lambda j, i, o, at: at + (pl.multiple_of(o + j * tn, 32), 0),
                                                  o=off, at=at))
        else:
            blk = (w.shape[-2], tn)
            spec = pl.BlockSpec((None,) * len(at) + blk,
                                functools.partial(lambda j, i, o, at: at + (0, j + o), o=off, at=at))
        in_specs.append(spec)
        if w.dtype != BF16:
            scratch.append(pltpu.VMEM(blk, BF16))
        needs_cast.append(w.dtype != BF16)
        args.append(w)
    for e, spec in extras:
        in_specs.append(spec)
        args.append(e)
    return pl.pallas_call(
        functools.partial(_mm_kernel, n_a=len(a_list), pairs=tuple(pairs), needs_cast=tuple(needs_cast),
                          staged=tuple(staged), n_extra=len(extras), transposed=transposed, epilogue=epilogue),
        grid=(n_out // tn, t // tm),
        in_specs=in_specs,
        out_specs=pl.BlockSpec((tm, tn), lambda j, i: (i, j)),
        out_shape=jax.ShapeDtypeStruct((t, n_out), out_dtype),
        scratch_shapes=scratch,
        compiler_params=_params("arbitrary", "arbitrary"),
        name=name,
    )(*args)


def _tile_spec(tm, tn, off=0):
    return pl.BlockSpec((tm, tn), functools.partial(lambda j, i, o: (i, j + o), o=off))


def _row_spec(tn):
    return pl.BlockSpec((1, tn), lambda j, i: (0, j))


def _rs_spec(tm):
    return pl.BlockSpec((tm, LANES), lambda j, i: (i, 0))


def _lane_tile(rs, n):
    return rs if n == LANES else jnp.concatenate([rs] * (n // LANES), axis=1)


def _ep_scaled(accs, extras):
    return accs[0] * _lane_tile(extras[0], accs[0].shape[1])


def _ep_swiglu(accs, extras):
    rs = _lane_tile(extras[0], accs[0].shape[1])
    return _silu(accs[0] * rs) * (accs[1] * rs)


def _ep_gated_sum(accs, extras):
    gates = [_sigmoid(e.astype(F32)) for e in extras]
    return gates[0] * accs[0] + gates[1] * accs[1] + gates[2] * accs[2]


def _ep_qk_norm_rotary(accs, extras):
    acc = accs[0]
    gain, cosf, sinf, rs = extras
    outs = []
    for c in range(acc.shape[1] // LANES):
        blk = acc[:, c * LANES:(c + 1) * LANES] * rs
        ms = jnp.mean(blk * blk, axis=-1, keepdims=True)
        y = blk * lax.rsqrt(ms + NORM_EPS) * gain[:, c * LANES:(c + 1) * LANES]
        outs.append(y * cosf + pltpu.roll(y, LANES // 2, 1) * sinf)
    return jnp.concatenate(outs, axis=-1)


def _residual_kernel(a_ref, w_ref, x_ref, g_ref, o_ref, xg_ref, rs_ref, wbf_ref, ss_ref, *, scale, d_model):
    j, i = pl.program_id(0), pl.program_id(1)
    tm, tn = o_ref.shape

    @pl.when(i == 0)
    def _():
        wbf_ref[...] = w_ref[...].astype(BF16)

    xn = x_ref[...] + scale * _dot(a_ref[...], wbf_ref[...])
    o_ref[...] = xn
    xg_ref[...] = (xn * g_ref[...]).astype(xg_ref.dtype)
    sq = xn * xn
    part = sq[:, 0:LANES]
    for c in range(1, tn // LANES):
        part = part + sq[:, c * LANES:(c + 1) * LANES]
    part = jnp.broadcast_to(jnp.sum(part, axis=-1, keepdims=True), (tm, LANES))
    rows = pl.ds(pl.multiple_of(i * tm, tm), tm)

    @pl.when(j == 0)
    def _():
        ss_ref[rows, :] = part

    @pl.when(j != 0)
    def _():
        ss_ref[rows, :] = ss_ref[rows, :] + part

    rs_ref[...] = lax.rsqrt(ss_ref[rows, :] * (1.0 / d_model) + NORM_EPS)


def _residual_matmul(name, layer, a, w, x, gain_next, scale, tm, tn):
    t, d = x.shape
    k_dim = w.shape[1]
    x_new, xg, rs_partial = pl.pallas_call(
        functools.partial(_residual_kernel, scale=scale, d_model=d),
        grid=(d // tn, t // tm),
        in_specs=[pl.BlockSpec((tm, k_dim), lambda j, i: (i, 0)),
                  pl.BlockSpec((None, k_dim, tn), lambda j, i: (layer, 0, j)),
                  pl.BlockSpec((tm, tn), lambda j, i: (i, j)),
                  pl.BlockSpec((1, tn), lambda j, i: (0, j))],
        out_specs=[pl.BlockSpec((tm, tn), lambda j, i: (i, j)),
                   pl.BlockSpec((tm, tn), lambda j, i: (i, j)),
                   pl.BlockSpec((None, tm, LANES), lambda j, i: (j, i, 0))],
        out_shape=[jax.ShapeDtypeStruct((t, d), F32), jax.ShapeDtypeStruct((t, d), BF16),
                   jax.ShapeDtypeStruct((d // tn, t, LANES), F32)],
        scratch_shapes=[pltpu.VMEM((k_dim, tn), BF16), pltpu.VMEM((t, LANES), F32)],
        compiler_params=_params("arbitrary", "arbitrary"),
        name=name,
    )(a, w, x, gain_next.reshape(1, d))
    return x_new, xg, rs_partial[d // tn - 1]


def _ffn(layer, x, xg, rs, w_in, w_out, gain_next):
    f = w_out.shape[1]
    tm, tf = 1024, 512
    h = _matmul("ffn_in", layer, [xg], [(w_in, 0), (w_in, f // tf)], (0, 0), _ep_swiglu,
                f, BF16, tm, tf, extras=[(rs, _rs_spec(tm))], stage_weights=True)
    return _residual_matmul("ffn_out", layer, h, w_out, x, gain_next, 0.5, 512, 1024)


def _attn_kernel(q_ref, k_ref, v_ref, lam_ref, gain_ref, linit_ref, o_ref):
    dh = ATTN_HEAD_DIM
    v = v_ref[...]
    lv = lam_ref[...]
    lam_init = linit_ref[...]
    s01 = jnp.sum(lv[0:1] * lv[1:2], axis=-1, keepdims=True)
    s23 = jnp.sum(lv[2:3] * lv[3:4], axis=-1, keepdims=True)
    lam = jnp.exp(s01) - jnp.exp(s23) + lam_init

    def scores(rows):
        return [_dot_nt(q_ref[rows, m * dh:(m + 1) * dh], k_ref[:, m * dh:(m + 1) * dh]) for m in range(2)]

    def softmax(s):
        p = [jnp.exp2(s[m] - jnp.max(s[m], axis=-1, keepdims=True)) for m in range(2)]
        return p, [1.0 / jnp.sum(p[m], axis=-1, keepdims=True) for m in range(2)]

    def values(p, r):
        o1, o2 = [_dot(p[m].astype(BF16), v) for m in range(2)]
        return o1 * r[0] - o2 * (lam * r[1])

    rows = q_ref.shape[0] // ATTN_ROW_GROUPS
    groups = [slice(g * rows, (g + 1) * rows) for g in range(ATTN_ROW_GROUPS)]
    s_next = scores(groups[0])
    outs = []
    for g in range(ATTN_ROW_GROUPS):
        s_cur = s_next
        if g + 1 < ATTN_ROW_GROUPS:
            s_next = scores(groups[g + 1])
        outs.append(values(*softmax(s_cur)))
    o = jnp.concatenate(outs, axis=0)
    ms = jnp.mean(o * o, axis=-1, keepdims=True)
    o = o * lax.rsqrt(ms + NORM_EPS) * gain_ref[...] * (1.0 - lam_init)
    o_ref[...] = o.astype(o_ref.dtype)


def _attention(qk, v, lam_vec, sub_gain, lam_init, batch, seq, tq=2048):
    t = qk.shape[0]
    hd = 2 * ATTN_HEAD_DIM
    nq = seq // tq
    return pl.pallas_call(
        _attn_kernel,
        grid=(batch, ATTN_HEADS, nq),
        in_specs=[
            pl.BlockSpec((tq, hd), lambda b, h, i: (b * nq + i, h)),
            pl.BlockSpec((seq, hd), lambda b, h, i: (b, ATTN_HEADS + h)),
            pl.BlockSpec((seq, ATTN_V_DIM), lambda b, h, i: (b, h)),
            pl.BlockSpec((4, ATTN_HEAD_DIM), lambda b, h, i: (0, 0)),
            pl.BlockSpec((1, ATTN_V_DIM), lambda b, h, i: (0, 0)),
            pl.BlockSpec((1, 1), lambda b, h, i: (0, 0)),
        ],
        out_specs=pl.BlockSpec((tq, ATTN_V_DIM), lambda b, h, i: (b * nq + i, h)),
        out_shape=jax.ShapeDtypeStruct((t, ATTN_HEADS * ATTN_V_DIM), BF16),
        compiler_params=_params("parallel", "parallel", "arbitrary"),
        name="diff_attention",
    )(qk, qk, v, lam_vec, sub_gain.reshape(1, -1), lam_init.reshape(1, 1))


def _scan_constants(c, rev):
    idx = np.arange(c)
    if rev:
        cum = idx[None, :] >= idx[:, None]
    else:
        cum = idx[None, :] <= idx[:, None]
    valids = [np.eye(c, dtype=np.float32)]
    s = c // 2
    while s >= 1:
        blk = idx // (2 * s)
        late = (idx // s) % 2
        same = blk[:, None] == blk[None, :]
        if not rev:
            valid = same & (late[:, None] == 1) & (late[None, :] == 0)
        else:
            valid = same & (late[:, None] == 0) & (late[None, :] == 1)
        valids.append(valid.astype(np.float32))
        s //= 2
    return (jnp.asarray(cum.astype(np.float32), dtype=BF16),
            jnp.asarray(np.stack(valids), dtype=F32))


def _level_masks(c, dk, n_lvl, rev):
    row = lax.broadcasted_iota(jnp.int32, (c, dk), 0)
    masks = []
    for lvl in range(1, n_lvl + 1):
        late = ((row >> (n_lvl - lvl)) & 1) == 1
        masks.append(jnp.logical_not(late) if rev else late)
    return masks


def _mid_row(cum, s):
    c, dk = cum.shape
    if s >= 8:
        parts = [jnp.broadcast_to(cum[b * 2 * s + s:b * 2 * s + s + 1, :], (2 * s, dk))
                 for b in range(c // (2 * s))]
        return parts[0] if len(parts) == 1 else jnp.concatenate(parts, axis=0)
    x3 = cum.reshape(c // 8, 8, dk)
    sub = lax.broadcasted_iota(jnp.int32, x3.shape, 1)

    def row(r):
        return jnp.broadcast_to(x3[:, r:r + 1, :], x3.shape)

    if s == 4:
        r3 = row(4)
    elif s == 2:
        r3 = jnp.where(sub < 4, row(2), row(6))
    else:
        r3 = jnp.where(sub < 2, row(1), jnp.where(sub < 4, row(3), jnp.where(sub < 6, row(5), row(7))))
    return r3.reshape(c, dk)


def _scan_chunk(q, k, v, g, st, cum_ref, valid_ref, masks, rev):
    c = q.shape[0]
    n_lvl = len(masks)
    tri = cum_ref[...]
    gh, gm, gl = _split3(g)
    cum = _dot(tri, gh) + _dot(tri, gm) + _dot(tri, gl)
    total = cum[0:1] if rev else cum[c - 1:c]
    a = valid_ref[0] * _dot_nt(q.astype(BF16), k.astype(BF16))
    for lvl in range(1, n_lvl + 1):
        e = -jnp.abs(cum - _mid_row(cum, c >> lvl))
        x = (jnp.where(masks[lvl - 1], q, k) * jnp.exp2(e)).astype(BF16)
        a = a + valid_ref[lvl] * _dot_nt(x, x)
    qd = (q * jnp.exp2(cum)).astype(BF16)
    kd = (k * jnp.exp2(total - cum)).astype(BF16)
    vb = v.astype(BF16)
    o = _dot(a.astype(BF16), vb) + _dot_nt(qd, st.astype(BF16))
    return o, st * jnp.exp2(total) + _dot_tn(vb, kd)


def _gla_gate(lr, w2, bias):
    lr_hi = lr.astype(BF16)
    lr_lo = (lr - lr_hi.astype(F32)).astype(BF16)
    w2_hi = w2.astype(BF16)
    w2_lo = (w2 - w2_hi.astype(F32)).astype(BF16)
    z = _dot(lr_hi, w2_hi) + _dot(lr_hi, w2_lo) + _dot(lr_lo, w2_hi) + bias
    return _log_sigmoid(z) * (LOG2_E / GLA_GATE_NORMALIZER)


def _gla_kernel(qf_ref, kf_ref, vf_ref, lrf_ref, qb_ref, kb_ref, vb_ref, lrb_ref,
                w2f_ref, bf_ref, w2b_ref, bb_ref, dallf_ref, validf_ref, dallb_ref, validb_ref,
                of_ref, ob_ref, stf_ref, stb_ref, *, chunk):
    @pl.when(pl.program_id(2) == 0)
    def _():
        stf_ref[...] = jnp.zeros_like(stf_ref)
        stb_ref[...] = jnp.zeros_like(stb_ref)

    r = GLA_GATE_RANK
    gf_all = _gla_gate(lrf_ref[...][:, 0:r], w2f_ref[...], bf_ref[...])
    gb_all = _gla_gate(lrb_ref[...][:, r:2 * r], w2b_ref[...], bb_ref[...])
    n_lvl = validf_ref.shape[0] - 1
    mf = _level_masks(chunk, GLA_K_DIM, n_lvl, False)
    mb = _level_masks(chunk, GLA_K_DIM, n_lvl, True)
    scale = GLA_K_DIM ** -0.5
    n = qf_ref.shape[0] // chunk
    stf, stb = stf_ref[...], stb_ref[...]
    for ci in range(n):
        sf = slice(ci * chunk, (ci + 1) * chunk)
        sb = slice((n - 1 - ci) * chunk, (n - ci) * chunk)
        o, stf = _scan_chunk(qf_ref[sf, :] * scale, kf_ref[sf, :], vf_ref[sf, :], gf_all[sf, :],
                             stf, dallf_ref, validf_ref, mf, False)
        of_ref[sf, :] = o
        o, stb = _scan_chunk(qb_ref[sb, :] * scale, kb_ref[sb, :], vb_ref[sb, :], gb_all[sb, :],
                             stb, dallb_ref, validb_ref, mb, True)
        ob_ref[sb, :] = o
    stf_ref[...] = stf
    stb_ref[...] = stb


def _hgrn_lower_bound(logits, lmask):
    pe = jnp.exp(logits - jnp.max(logits, axis=0, keepdims=True))
    p = pe / jnp.sum(pe, axis=0, keepdims=True)
    lb = jnp.sum(p * lmask, axis=0, keepdims=True) - p[0:1]
    lb = jnp.clip(lb, 0.0, 1.0 - 1e-6)
    return lb, jnp.log(jnp.maximum(lb, LB_FLOOR)), jnp.log1p(-lb)


def _hgrn_gate(z, lb, log_lb, log_1m):
    t2 = log_1m + _log_sigmoid(z)
    g = jnp.maximum(log_lb, t2) + jnp.log1p(jnp.exp(-jnp.abs(log_lb - t2)))
    return (1.0 - lb) * _sigmoid(-z), g * LOG2_E


def _hgrn_kernel(qf_ref, zf_ref, vf_ref, qb_ref, zb_ref, vb_ref, lbf_ref, lbb_ref, lmask_ref,
                 dallf_ref, validf_ref, dallb_ref, validb_ref, of_ref, ob_ref, stf_ref, stb_ref,
                 *, chunk):
    @pl.when(pl.program_id(2) == 0)
    def _():
        stf_ref[...] = jnp.zeros_like(stf_ref)
        stb_ref[...] = jnp.zeros_like(stb_ref)

    lbf = _hgrn_lower_bound(lbf_ref[...], lmask_ref[...])
    lbb = _hgrn_lower_bound(lbb_ref[...], lmask_ref[...])
    n_lvl = validf_ref.shape[0] - 1
    mf = _level_masks(chunk, HGRN_K_DIM, n_lvl, False)
    mb = _level_masks(chunk, HGRN_K_DIM, n_lvl, True)
    scale = HGRN_K_DIM ** -0.5
    n = qf_ref.shape[0] // chunk
    stf, stb = stf_ref[...], stb_ref[...]
    for ci in range(n):
        sf = slice(ci * chunk, (ci + 1) * chunk)
        sb = slice((n - 1 - ci) * chunk, (n - ci) * chunk)
        k, g = _hgrn_gate(zf_ref[sf, :], *lbf)
        o, stf = _scan_chunk(qf_ref[sf, :] * scale, k, vf_ref[sf, :], g,
                             stf, dallf_ref, validf_ref, mf, False)
        of_ref[sf, :] = o
        k, g = _hgrn_gate(zb_ref[sb, :], *lbb)
        o, stb = _scan_chunk(qb_ref[sb, :] * scale, k, vb_ref[sb, :], g,
                             stb, dallb_ref, validb_ref, mb, True)
        ob_ref[sb, :] = o
    stf_ref[...] = stf
    stb_ref[...] = stb


def _fwd_idx(nblk, col):
    return lambda b, h, c: (b * nblk + c, col(h))


def _bwd_idx(nblk, col):
    return lambda b, h, c: (b * nblk + (nblk - 1 - c), col(h))


def _const_spec(arr):
    zeros = (0,) * arr.ndim
    return pl.BlockSpec(arr.shape, lambda b, h, c: zeros)


def _gla_scan(p_gla, lr, w2f, bf, w2b, bb, batch, seq):
    t = p_gla.shape[0]
    tb, c = SCAN_BLOCK, SCAN_CHUNK
    nblk = seq // tb
    dk, dv, nh = GLA_K_DIM, GLA_V_DIM, GLA_HEADS
    consts = _scan_constants(c, False) + _scan_constants(c, True)
    tok_specs = []
    for idx in (_fwd_idx, _bwd_idx):
        tok_specs += [
            pl.BlockSpec((tb, dk), idx(nblk, lambda h: h)),
            pl.BlockSpec((tb, dk), idx(nblk, lambda h: nh + h)),
            pl.BlockSpec((tb, dv), idx(nblk, lambda h: (2 * nh * dk) // dv + h)),
            pl.BlockSpec((tb, LANES), idx(nblk, lambda h: (2 * nh * dk + 2 * nh * dv) // LANES)),
        ]
    gate_specs = [pl.BlockSpec((GLA_GATE_RANK, dk), lambda b, h, c_: (0, h)),
                  pl.BlockSpec((1, dk), lambda b, h, c_: (0, h))] * 2
    out_sd = jax.ShapeDtypeStruct((t, nh * dv), F32)
    return pl.pallas_call(
        functools.partial(_gla_kernel, chunk=c),
        grid=(batch, nh, nblk),
        in_specs=tok_specs + gate_specs + [_const_spec(a) for a in consts],
        out_specs=[pl.BlockSpec((tb, dv), _fwd_idx(nblk, lambda h: h)),
                   pl.BlockSpec((tb, dv), _bwd_idx(nblk, lambda h: h))],
        out_shape=[out_sd, out_sd],
        scratch_shapes=[pltpu.VMEM((dv, dk), F32), pltpu.VMEM((dv, dk), F32)],
        compiler_params=_params("parallel", "parallel", "arbitrary"),
        name="gla_scan",
    )(p_gla, p_gla, p_gla, lr, p_gla, p_gla, p_gla, lr,
      w2f, bf.reshape(1, -1), w2b, bb.reshape(1, -1), *consts)


def _hgrn_scan(p_hg, lb_fwd, lb_bwd, lmask, batch, seq):
    t = p_hg.shape[0]
    tb, c = SCAN_BLOCK, SCAN_CHUNK
    nblk = seq // tb
    dk, dv, nh = HGRN_K_DIM, HGRN_V_DIM, HGRN_HEADS
    consts = _scan_constants(c, False) + _scan_constants(c, True)
    n_layers = lb_fwd.shape[0]
    tok_specs = []
    for idx, z_col in ((_fwd_idx, nh), (_bwd_idx, 2 * nh)):
        tok_specs += [
            pl.BlockSpec((tb, dk), idx(nblk, lambda h: h)),
            pl.BlockSpec((tb, dk), idx(nblk, functools.partial(lambda h, z: z + h, z=z_col))),
            pl.BlockSpec((tb, dv), idx(nblk, lambda h: (3 * nh * dk) // dv + h)),
        ]
    lb_specs = [pl.BlockSpec((n_layers, dk), lambda b, h, c_: (0, h)),
                pl.BlockSpec((n_layers, dk), lambda b, h, c_: (0, h)),
                pl.BlockSpec((n_layers, 1), lambda b, h, c_: (0, 0))]
    out_sd = jax.ShapeDtypeStruct((t, nh * dv), F32)
    return pl.pallas_call(
        functools.partial(_hgrn_kernel, chunk=c),
        grid=(batch, nh, nblk),
        in_specs=tok_specs + lb_specs + [_const_spec(a) for a in consts],
        out_specs=[pl.BlockSpec((tb, dv), _fwd_idx(nblk, lambda h: h)),
                   pl.BlockSpec((tb, dv), _bwd_idx(nblk, lambda h: h))],
        out_shape=[out_sd, out_sd],
        scratch_shapes=[pltpu.VMEM((dv, dk), F32), pltpu.VMEM((dv, dk), F32)],
        compiler_params=_params("parallel", "parallel", "arbitrary"),
        name="hgrn_scan",
    )(p_hg, p_hg, p_hg, p_hg, p_hg, p_hg, lb_fwd, lb_bwd, lmask, *consts)


def _headnorm_gate_kernel(of_ref, ob_ref, r_ref, gain_ref, o_ref, *, hd):
    o = of_ref[...] + ob_ref[...]
    gain = gain_ref[...]
    outs = []
    for h in range(o.shape[1] // hd):
        blk = o[:, h * hd:(h + 1) * hd]
        ms = jnp.mean(blk * blk, axis=-1, keepdims=True)
        outs.append(blk * lax.rsqrt(ms + NORM_EPS) * gain)
    y = jnp.concatenate(outs, axis=-1) * _silu(r_ref[...])
    o_ref[...] = y.astype(o_ref.dtype)


def _headnorm_gate(o_f, o_b, p, r_off, gain, hd, tm=512):
    t, w = o_f.shape
    return pl.pallas_call(
        functools.partial(_headnorm_gate_kernel, hd=hd),
        grid=(t // tm,),
        in_specs=[pl.BlockSpec((tm, w), lambda i: (i, 0)),
                  pl.BlockSpec((tm, w), lambda i: (i, 0)),
                  pl.BlockSpec((tm, w), lambda i: (i, r_off)),
                  pl.BlockSpec((1, hd), lambda i: (0, 0))],
        out_specs=pl.BlockSpec((tm, w), lambda i: (i, 0)),
        out_shape=jax.ShapeDtypeStruct((t, w), BF16),
        compiler_params=_params("parallel"),
        name="headnorm_gate",
    )(o_f, o_b, p, gain.reshape(1, hd))


def _rotary_tables(seq):
    half = ATTN_HEAD_DIM // 2
    inv_freq = ROPE_THETA ** (-jnp.arange(half, dtype=F32) / half)
    ang = jnp.arange(seq, dtype=jnp.int32).astype(F32)[:, None] * inv_freq[None, :]
    cos, sin = jnp.cos(ang), jnp.sin(ang)
    return jnp.concatenate([cos, cos], axis=-1), jnp.concatenate([-sin, sin], axis=-1)


def kernel(x, ffn1_norm, ffn1_w_in, ffn1_w_out, mix_norm, w_in, attn_q_norm, attn_k_norm, attn_lambda, attn_sub_norm, gla_gate_w2_fwd, gla_gate_b_fwd, gla_gate_w2_bwd, gla_gate_b_bwd, gla_out_norm, hgrn_lb_fwd, hgrn_lb_bwd, hgrn_out_norm, w_branch_attn, w_branch_gla, w_branch_hgrn, w_out, ffn2_norm, ffn2_w_in, ffn2_w_out):
    batch, seq, d = x.shape
    t = batch * seq
    depth = w_in.shape[0]
    cosf, sinf = _rotary_tables(seq)

    qk_w = 2 * ATTN_HEADS * 2 * ATTN_HEAD_DIM
    av_w = ATTN_HEADS * ATTN_V_DIM
    gla_w = 2 * GLA_HEADS * GLA_K_DIM + 2 * GLA_HEADS * GLA_V_DIM
    lr_w = 2 * GLA_GATE_RANK
    hg_w = 3 * HGRN_HEADS * HGRN_K_DIM + 2 * HGRN_HEADS * HGRN_V_DIM
    c_av, c_gla = qk_w, qk_w + av_w
    c_lr = c_gla + gla_w
    c_tail = c_lr + lr_w

    lmasks = np.tril(np.ones((depth, depth), np.float32))[:, :, None]

    tm, tn = 1024, 512
    tm_p, tn_p = 1024, 1024
    tm_w, tn_w = 512, 2048
    q_scale = ATTN_HEAD_DIM ** -0.5 * LOG2_E
    n_sub = qk_w // (2 * ATTN_HEAD_DIM)
    nrow = seq // tm_w
    rot_spec = pl.BlockSpec((tm_w, LANES), lambda j, i: (i % nrow, 0))

    w_in_t = jnp.swapaxes(w_in, 1, 2)
    w_qk_t = w_in_t[:, :qk_w, :].astype(BF16)

    xc = x.reshape(t, d)
    xg, rs = _rmsnorm(xc, ffn1_norm[0])
    for l in range(depth):
        lam_init = jnp.full((1, 1), 0.8 - 0.6 * math.exp(-0.3 * l), F32)
        xc, hn, hrs = _ffn(l, xc, xg, rs, ffn1_w_in, ffn1_w_out, mix_norm[l])

        qk_gain = jnp.concatenate([jnp.tile(attn_q_norm[l] * q_scale, n_sub),
                                   jnp.tile(attn_k_norm[l], n_sub)]).reshape(1, qk_w)
        qk = _matmul("proj_qk", l, [hn], [(w_qk_t, 0)], (0,), _ep_qk_norm_rotary,
                     qk_w, BF16, tm_w, tn_w, transposed=True,
                     extras=[(qk_gain, _row_spec(tn_w)), (cosf, rot_spec), (sinf, rot_spec),
                             (hrs, _rs_spec(tm_w))])
        proj = functools.partial(_matmul, layer=l, a_list=[hn], pairs=(0,), epilogue=_ep_scaled,
                                 tm=tm_p, tn=tn_p, transposed=True, stage_weights=True,
                                 extras=[(hrs, _rs_spec(tm_p))])
        p_av = proj("proj_av", w_list=[(w_in_t, c_av)], n_out=av_w, out_dtype=BF16)
        p_gla = proj("proj_gla", w_list=[(w_in_t, c_gla)], n_out=gla_w + LANES, out_dtype=F32,
                     tn=(gla_w + LANES) // 5)
        p_hg = proj("proj_hgrn", w_list=[(w_in_t, c_tail)], n_out=hg_w, out_dtype=F32)
        p_gate = proj("proj_gate", w_list=[(w_in_t, c_tail + hg_w)], n_out=3 * d, out_dtype=BF16)

        o_a = _attention(qk, p_av, attn_lambda[l], attn_sub_norm[l], lam_init, batch, seq)

        og_f, og_b = _gla_scan(p_gla, p_gla, gla_gate_w2_fwd[l], gla_gate_b_fwd[l],
                               gla_gate_w2_bwd[l], gla_gate_b_bwd[l], batch, seq)
        u_g = _headnorm_gate(og_f, og_b, p_gla, gla_w // (GLA_HEADS * GLA_V_DIM) - 1, gla_out_norm[l],
                             GLA_V_DIM)

        oh_f, oh_b = _hgrn_scan(p_hg, hgrn_lb_fwd, hgrn_lb_bwd, jnp.asarray(lmasks[l]), batch, seq)
        u_h = _headnorm_gate(oh_f, oh_b, p_hg, hg_w // (HGRN_HEADS * HGRN_V_DIM) - 1, hgrn_out_norm[l],
                             HGRN_V_DIM)

        merged = _matmul(
            "merge", l, [o_a, u_g, u_h],
            [(w_branch_attn, 0), (w_branch_gla, 0), (w_branch_hgrn, 0)], (0, 1, 2),
            _ep_gated_sum, d, BF16, tm, tn,
            extras=[(p_gate, _tile_spec(tm, tn, b * d // tn)) for b in range(3)])
        xc, xg, rs = _residual_matmul("out_proj", l, merged, w_out, xc, ffn2_norm[l], 1.0, tm, tn)

        xc, xg, rs = _ffn(l, xc, xg, rs, ffn2_w_in, ffn2_w_out, ffn1_norm[(l + 1) % depth])
    return xc.reshape(batch, seq, d)
```
